```python
import math
import jax, jax.numpy as jnp
from jax import lax
import numpy as np

D_MODEL = 1024
BATCH = 4
SEQ = 4096
DEPTH = 1

CHUNK = 64
N_MEM = 256
D_CONV = D_MODEL
CONV_K = 31
D_SSM = D_MODEL // 2
SSM_GROUP = 16
SSM_GROUPS = D_SSM // SSM_GROUP
SSM_STATE = 64
XA_HEADS = 4
XA_HEAD_DIM = D_MODEL // XA_HEADS
D_FF = 4 * D_MODEL
D_IN = 2 * D_CONV + D_SSM + 2 * D_MODEL
LN_EPS = 1e-5
DEEPNORM_ALPHA = (2.0 * DEPTH) ** 0.25
DEEPNORM_BETA = (8.0 * DEPTH) ** -0.25

kernel_name = "gated_conformer_s5_memxattn_deepnorm"


def layer_norm(x, g, b):
    xf = x.astype(jnp.float32)
    mu = jnp.mean(xf, axis=-1, keepdims=True)
    xc = xf - mu
    var = jnp.mean(xc * xc, axis=-1, keepdims=True)
    y = xc * lax.rsqrt(var + LN_EPS) * g.astype(jnp.float32) + b.astype(jnp.float32)
    return y.astype(x.dtype)


def conformer_conv_branch(val, gate, dw, db, ng, nb, w_out):
    u = val * jax.nn.sigmoid(gate)
    c = lax.conv_general_dilated(
        u, dw[:, None, :].astype(u.dtype),
        window_strides=(1,), padding=[(CONV_K - 1, 0)],
        dimension_numbers=("NWC", "WIO", "NWC"),
        feature_group_count=D_CONV) + db
    c = layer_norm(c, ng, nb)
    c = jax.nn.silu(c)
    return c @ w_out


def _ssm_combine(left, right):
    a1r, a1i, b1r, b1i = left
    a2r, a2i, b2r, b2i = right
    ar = a2r * a1r - a2i * a1i
    ai = a2r * a1i + a2i * a1r
    br = a2r * b1r - a2i * b1i + b2r
    bi = a2r * b1i + a2i * b1r + b2i
    return (ar, ai, br, bi)


def s5_branch(u, log_step, lam_re, lam_im, b_re, b_im, c_re, c_im, d, w_glu):
    bsz, s, _ = u.shape
    uf = u.astype(jnp.float32).reshape(bsz, s, SSM_GROUPS, SSM_GROUP)
    step = jnp.exp(log_step.astype(jnp.float32))[:, None]
    lr = lam_re.astype(jnp.float32)
    li = lam_im.astype(jnp.float32)
    mag = jnp.exp(lr * step)
    ar = mag * jnp.cos(li * step)
    ai = mag * jnp.sin(li * step)
    den = lr * lr + li * li
    nr = ar - 1.0
    cr = (nr * lr + ai * li) / den
    ci = (ai * lr - nr * li) / den
    br = b_re.astype(jnp.float32)
    bi = b_im.astype(jnp.float32)
    bbr = cr[..., None] * br - ci[..., None] * bi
    bbi = cr[..., None] * bi + ci[..., None] * br
    bu_r = jnp.einsum('bsgh,gph->bsgp', uf, bbr)
    bu_i = jnp.einsum('bsgh,gph->bsgp', uf, bbi)
    a_r = jnp.broadcast_to(ar, bu_r.shape)
    a_i = jnp.broadcast_to(ai, bu_i.shape)
    _, _, xr, xi = lax.associative_scan(_ssm_combine, (a_r, a_i, bu_r, bu_i), axis=1)
    y = (jnp.einsum('bsgp,ghp->bsgh', xr, c_re.astype(jnp.float32))
         - jnp.einsum('bsgp,ghp->bsgh', xi, c_im.astype(jnp.float32))
         + d.astype(jnp.float32).reshape(SSM_GROUPS, SSM_GROUP) * uf)
    y = y.reshape(bsz, s, D_SSM).astype(u.dtype)
    z = y @ w_glu
    return z[..., :D_MODEL] * jax.nn.sigmoid(z[..., D_MODEL:])


def hybrid_mixer(h, w_in, conv_dw, conv_db, conv_norm_g, conv_norm_b, w_conv_out,
                 log_step, lam_re, lam_im, b_re, b_im, c_re, c_im, d, w_ssm_glu, w_mix_out):
    p = h @ w_in
    o0 = D_CONV
    o1 = 2 * D_CONV
    o2 = o1 + D_SSM
    o3 = o2 + D_MODEL
    conv_val, conv_gate = p[..., :o0], p[..., o0:o1]
    ssm_in = p[..., o1:o2]
    gate_a, gate_b = p[..., o2:o3], p[..., o3:]
    y_a = conformer_conv_branch(conv_val, conv_gate, conv_dw, conv_db,
                                conv_norm_g, conv_norm_b, w_conv_out)
    y_b = s5_branch(ssm_in, log_step, lam_re, lam_im, b_re, b_im, c_re, c_im, d, w_ssm_glu)
    merged = jax.nn.sigmoid(gate_a) * y_a + jax.nn.sigmoid(gate_b) * y_b
    return merged @ w_mix_out


def memory_cross_attention(h, mem, wq, wkv, wo):
    bsz, s, _ = h.shape
    q = (h @ wq).reshape(bsz, s, XA_HEADS, XA_HEAD_DIM)
    kv = mem @ wkv
    k = kv[..., :D_MODEL].reshape(bsz, N_MEM, XA_HEADS, XA_HEAD_DIM)
    v = kv[..., D_MODEL:].reshape(bsz, N_MEM, XA_HEADS, XA_HEAD_DIM)
    scores = jnp.einsum('bshd,bmhd->bhsm', q.astype(jnp.float32), k.astype(jnp.float32))
    probs = jax.nn.softmax(scores * (XA_HEAD_DIM ** -0.5), axis=-1).astype(h.dtype)
    o = jnp.einsum('bhsm,bmhd->bshd', probs, v).reshape(bsz, s, D_MODEL)
    return o @ wo


def sq_relu_mlp(h, w_up, w_down):
    z = jax.nn.relu(h @ w_up)
    return (z * z) @ w_down


def setup_inputs(seed: int = 0) -> dict:
    key = jax.random.key(seed)
    ks = jax.random.split(key, 32)
    f32 = jnp.float32
    L = DEPTH
    nrm = lambda k, shape, scale: jax.random.normal(k, shape, f32) * scale
    gain = lambda k, shape: 1.0 + 0.02 * jax.random.normal(k, shape, f32)
    bias = lambda k, shape: 0.02 * jax.random.normal(k, shape, f32)
    n_idx = jnp.arange(SSM_STATE, dtype=f32)
    lam_re = -0.5 + 0.01 * jax.random.normal(ks[11], (L, SSM_GROUPS, SSM_STATE), f32)
    lam_im = math.pi * n_idx + 0.01 * jax.random.normal(ks[12], (L, SSM_GROUPS, SSM_STATE), f32)
    log_step = jax.random.uniform(ks[10], (L, SSM_GROUPS), f32,
                                  math.log(1e-3), math.log(1e-1))
    return {
        "x": jax.random.normal(ks[0], (BATCH, SEQ, D_MODEL), f32),
        "mem": jax.random.normal(ks[1], (BATCH, N_MEM, D_MODEL), f32),
        "in_norm_g": gain(ks[2], (D_MODEL,)),
        "in_norm_b": bias(ks[3], (D_MODEL,)),
        "w_in": nrm(ks[4], (L, D_MODEL, D_IN), D_MODEL ** -0.5),
        "conv_dw": nrm(ks[5], (L, CONV_K, D_CONV), CONV_K ** -0.5),
        "conv_db": bias(ks[6], (L, D_CONV)),
        "conv_norm_g": gain(ks[7], (L, D_CONV)),
        "conv_norm_b": bias(ks[8], (L, D_CONV)),
        "w_conv_out": nrm(ks[9], (L, D_CONV, D_MODEL), D_CONV ** -0.5),
        "ssm_log_step": log_step,
        "ssm_lambda_re": lam_re,
        "ssm_lambda_im": lam_im,
        "ssm_b_re": nrm(ks[13], (L, SSM_GROUPS, SSM_STATE, SSM_GROUP), (2.0 * SSM_GROUP) ** -0.5),
        "ssm_b_im": nrm(ks[14], (L, SSM_GROUPS, SSM_STATE, SSM_GROUP), (2.0 * SSM_GROUP) ** -0.5),
        "ssm_c_re": nrm(ks[15], (L, SSM_GROUPS, SSM_GROUP, SSM_STATE), (2.0 * SSM_STATE) ** -0.5),
        "ssm_c_im": nrm(ks[16], (L, SSM_GROUPS, SSM_GROUP, SSM_STATE), (2.0 * SSM_STATE) ** -0.5),
        "ssm_d": nrm(ks[17], (L, D_SSM), 1.0),
        "w_ssm_glu": nrm(ks[18], (L, D_SSM, 2 * D_MODEL), D_SSM ** -0.5),
        "w_mix_out": nrm(ks[19], (L, D_MODEL, D_MODEL), DEEPNORM_BETA * D_MODEL ** -0.5),
        "ln1_g": gain(ks[20], (L, D_MODEL)),
        "ln1_b": bias(ks[21], (L, D_MODEL)),
        "xa_wq": nrm(ks[22], (L, D_MODEL, D_MODEL), D_MODEL ** -0.5),
        "xa_wkv": nrm(ks[23], (L, D_MODEL, 2 * D_MODEL), D_MODEL ** -0.5),
        "xa_wo": nrm(ks[24], (L, D_MODEL, D_MODEL), DEEPNORM_BETA * D_MODEL ** -0.5),
        "ln2_g": gain(ks[25], (L, D_MODEL)),
        "ln2_b": bias(ks[26], (L, D_MODEL)),
        "mlp_w_up": nrm(ks[27], (L, D_MODEL, D_FF), D_MODEL ** -0.5),
        "mlp_w_down": nrm(ks[28], (L, D_FF, D_MODEL), DEEPNORM_BETA * D_FF ** -0.5),
        "ln3_g": gain(ks[29], (L, D_MODEL)),
        "ln3_b": bias(ks[30], (L, D_MODEL)),
    }


def reference(x, mem, in_norm_g, in_norm_b, w_in, conv_dw, conv_db, conv_norm_g, conv_norm_b,
              w_conv_out, ssm_log_step, ssm_lambda_re, ssm_lambda_im, ssm_b_re, ssm_b_im,
              ssm_c_re, ssm_c_im, ssm_d, w_ssm_glu, w_mix_out, ln1_g, ln1_b,
              xa_wq, xa_wkv, xa_wo, ln2_g, ln2_b, mlp_w_up, mlp_w_down, ln3_g, ln3_b):
    h = layer_norm(x, in_norm_g, in_norm_b)
    for l in range(DEPTH):
        mix = hybrid_mixer(h, w_in[l], conv_dw[l], conv_db[l], conv_norm_g[l], conv_norm_b[l],
                           w_conv_out[l], ssm_log_step[l], ssm_lambda_re[l], ssm_lambda_im[l],
                           ssm_b_re[l], ssm_b_im[l], ssm_c_re[l], ssm_c_im[l], ssm_d[l],
                           w_ssm_glu[l], w_mix_out[l])
        h = layer_norm(DEEPNORM_ALPHA * h + mix, ln1_g[l], ln1_b[l])
        xa = memory_cross_attention(h, mem, xa_wq[l], xa_wkv[l], xa_wo[l])
        h = layer_norm(DEEPNORM_ALPHA * h + xa, ln2_g[l], ln2_b[l])
        ff = sq_relu_mlp(h, mlp_w_up[l], mlp_w_down[l])
        h = layer_norm(DEEPNORM_ALPHA * h + ff, ln3_g[l], ln3_b[l])
    return h
```

```python
import functools

import jax
import jax.numpy as jnp
from jax import lax
from jax.experimental import pallas as pl
from jax.experimental.pallas import tpu as pltpu

F32 = jnp.float32
BF16 = jnp.bfloat16

D_MODEL = 1024
D_CONV = 1024
CONV_K = 31
D_SSM = 512
SSM_GROUP = 16
SSM_GROUPS = 32
SSM_STATE = 64
XA_HEADS = 4
XA_HEAD_DIM = 256
D_FF = 4096
LN_EPS = 1e-5
DEEPNORM_ALPHA = 2.0 ** 0.25

SSM_CHUNK = 64
CONV_HALO = 32
ROW_TILE = 512
CONV_TILE = 256
VMEM_LIMIT = 48 * 1024 * 1024


def _layer_norm(x, g, b):
    mu = jnp.mean(x, axis=-1, keepdims=True)
    xc = x - mu
    var = jnp.mean(xc * xc, axis=-1, keepdims=True)
    return xc * lax.rsqrt(var + LN_EPS) * g + b


def _dot(a, b):
    return jnp.dot(a, b, preferred_element_type=F32)


def _params(n_axes=1):
    return pltpu.CompilerParams(dimension_semantics=("parallel",) * n_axes,
                                vmem_limit_bytes=VMEM_LIMIT)


def _const_spec(shape):
    nd = len(shape)
    return pl.BlockSpec(shape, lambda *_: (0,) * nd)


def _in_proj_kernel(x_ref, g_ref, b_ref, w_ref, u_ref, s_ref, ga_ref, gb_ref):
    h = _layer_norm(x_ref[...], g_ref[...], b_ref[...]).astype(BF16)
    o0, o1, o2, o3 = D_CONV, 2 * D_CONV, 2 * D_CONV + D_SSM, 2 * D_CONV + D_SSM + D_MODEL
    val = _dot(h, w_ref[:, 0:o0])
    gate = _dot(h, w_ref[:, o0:o1])
    u_ref[...] = (val * jax.nn.sigmoid(gate)).astype(BF16)
    s_ref[...] = _dot(h, w_ref[:, o1:o2]).astype(BF16)
    ga_ref[...] = jax.nn.sigmoid(_dot(h, w_ref[:, o2:o3])).astype(BF16)
    gb_ref[...] = jax.nn.sigmoid(_dot(h, w_ref[:, o3:])).astype(BF16)


def _in_proj(x2, g, b, w_in):
    n = x2.shape[0]
    d_in = w_in.shape[1]
    row = lambda w: pl.BlockSpec((ROW_TILE, w), lambda i: (i, 0))
    return pl.pallas_call(
        _in_proj_kernel,
        grid=(n // ROW_TILE,),
        in_specs=[row(D_MODEL), _const_spec((1, D_MODEL)), _const_spec((1, D_MODEL)),
                  _const_spec((D_MODEL, d_in))],
        out_specs=[row(D_CONV), row(D_SSM), row(D_MODEL), row(D_MODEL)],
        out_shape=[jax.ShapeDtypeStruct((n, D_CONV), BF16), jax.ShapeDtypeStruct((n, D_SSM), BF16),
                   jax.ShapeDtypeStruct((n, D_MODEL), BF16), jax.ShapeDtypeStruct((n, D_MODEL), BF16)],
        compiler_params=_params(),
        name="in_proj",
    )(x2, g, b, w_in)


def _conv_kernel(tiles_per_seq, u_ref, halo_ref, dw_ref, db_ref, ng_ref, nb_ref, w_ref, ga_ref,
                 o_ref, ext_ref):
    first = (pl.program_id(0) % tiles_per_seq) == 0
    halo = halo_ref[...].astype(F32)
    ext_ref[0:CONV_HALO, :] = jnp.where(first, 0.0, halo)
    ext_ref[CONV_HALO:, :] = u_ref[...].astype(F32)
    base = CONV_HALO - (CONV_K - 1)
    acc = jnp.broadcast_to(db_ref[...], (CONV_TILE, D_CONV))
    for j in range(CONV_K):
        acc = acc + dw_ref[j:j + 1, :] * ext_ref[base + j:base + j + CONV_TILE, :]
    c = _layer_norm(acc, ng_ref[...], nb_ref[...])
    c = c * jax.nn.sigmoid(c)
    ya = _dot(c.astype(BF16), w_ref[...])
    o_ref[...] = (ga_ref[...].astype(F32) * ya).astype(BF16)


def _conv_branch(u, dw, db, ng, nb, w_out, ga, seq):
    n = u.shape[0]
    tiles_per_seq = seq // CONV_TILE
    halo_per_tile = CONV_TILE // CONV_HALO
    row = lambda w: pl.BlockSpec((CONV_TILE, w), lambda i: (i, 0))
    halo = pl.BlockSpec((CONV_HALO, D_CONV), lambda i: (jnp.maximum(i * halo_per_tile - 1, 0), 0))
    return pl.pallas_call(
        functools.partial(_conv_kernel, tiles_per_seq),
        grid=(n // CONV_TILE,),
        in_specs=[row(D_CONV), halo, _const_spec((CONV_K, D_CONV)), _const_spec((1, D_CONV)),
                  _const_spec((1, D_CONV)), _const_spec((1, D_CONV)), _const_spec((D_CONV, D_MODEL)),
                  row(D_MODEL)],
        out_specs=row(D_MODEL),
        out_shape=jax.ShapeDtypeStruct((n, D_MODEL), BF16),
        scratch_shapes=[pltpu.VMEM((CONV_HALO + CONV_TILE, D_CONV), F32)],
        compiler_params=_params(),
        name="conv_branch",
    )(u, u, dw, db, ng, nb, w_out, ga)


def _ssm_prep_kernel(lsc_ref, lrc_ref, lic_ref, lsr_ref, lrr_ref, lir_ref, btr_ref, bti_ref,
                     cer_ref, cei_ref, wbig_ref, w1_ref, lamt_ref):
    t = SSM_CHUNK
    p = SSM_STATE
    width = SSM_GROUP * t
    a_c = lrc_ref[0] * jnp.exp(lsc_ref[0])
    th_c = lic_ref[0] * jnp.exp(lsc_ref[0])
    j = (lax.broadcasted_iota(jnp.int32, (p, 128), 1) % t).astype(F32)
    mag = jnp.exp(a_c * j)
    l0r = mag * jnp.cos(th_c * j)
    l0i = mag * jnp.sin(th_c * j)
    reps = width // 128
    l0r = jnp.concatenate([l0r] * reps, axis=1)
    l0i = jnp.concatenate([l0i] * reps, axis=1)
    cer = cer_ref[0]
    cei = cei_ref[0]
    cl0r = cer * l0r - cei * l0i
    cl0i = cer * l0i + cei * l0r
    lam_r = jnp.exp(a_c) * jnp.cos(th_c)
    lam_i = jnp.exp(a_c) * jnp.sin(th_c)
    cl1r = cl0r * lam_r - cl0i * lam_i
    cl1i = cl0r * lam_i + cl0i * lam_r
    wbig_ref[0, width:width + p, :] = cl1r.astype(BF16)
    wbig_ref[0, width + p:width + 2 * p, :] = (-cl1i).astype(BF16)

    step_r = jnp.exp(lsr_ref[0])
    lr = lrr_ref[0]
    li = lir_ref[0]
    a_r = lr * step_r
    th_r = li * step_r
    ar = jnp.exp(a_r) * jnp.cos(th_r)
    ai = jnp.exp(a_r) * jnp.sin(th_r)
    den = lr * lr + li * li
    nr = ar - 1.0
    cr = (nr * lr + ai * li) / den
    ci = (ai * lr - nr * li) / den
    btr = btr_ref[0]
    bti = bti_ref[0]
    bbr = cr * btr - ci * bti
    bbi = cr * bti + ci * btr

    hp = lax.Precision.HIGHEST
    g2 = (jnp.dot(bbr[:, :p], cl0r, precision=hp, preferred_element_type=F32)
          - jnp.dot(bbi[:, :p], cl0i, precision=hp, preferred_element_type=F32))
    k_idx = lax.broadcasted_iota(jnp.int32, (t, width), 0)
    i_idx = lax.broadcasted_iota(jnp.int32, (t, width), 1) % t
    causal = i_idx >= k_idx
    for h in range(SSM_GROUP):
        taps = jnp.broadcast_to(g2[h:h + 1, :], (t, width))
        taps = pltpu.roll(taps, 0, 1, stride=1, stride_axis=0)
        wbig_ref[0, h * t:(h + 1) * t, :] = jnp.where(causal, taps, 0.0).astype(BF16)

    kk = ((t - 1) - lax.broadcasted_iota(jnp.int32, (t, 2 * p), 0)).astype(F32)
    magk = jnp.exp(a_r * kk)
    lkr = magk * jnp.cos(th_r * kk)
    lki = magk * jnp.sin(th_r * kk)
    real_half = lax.broadcasted_iota(jnp.int32, (t, 2 * p), 1) < p
    for h in range(SSM_GROUP):
        br = bbr[h:h + 1, :]
        bi = bbi[h:h + 1, :]
        w1_ref[0, h * t:(h + 1) * t, :] = jnp.where(real_half, lkr * br - lki * bi,
                                                    lkr * bi + lki * br).astype(BF16)

    magt = jnp.exp(a_r * float(t))
    ltr = magt * jnp.cos(th_r * float(t))
    lti = magt * jnp.sin(th_r * float(t))
    sub = lax.broadcasted_iota(jnp.int32, (8, 2 * p), 0)
    lamt_ref[0] = jnp.where(sub == 0, ltr, jnp.where(sub == 1, lti, 0.0))


def _ssm_prep(log_step, lam_re, lam_im, b_re, b_im, c_re, c_im):
    g, p, hh, t = SSM_GROUPS, SSM_STATE, SSM_GROUP, SSM_CHUNK
    width = hh * t
    col = lambda v: v[:, :, None]
    row2 = lambda v: jnp.concatenate([v, v], axis=-1)[:, None, :]
    ls = jnp.broadcast_to(log_step[:, None], (g, p))
    bt = lambda b: jnp.concatenate([b.transpose(0, 2, 1)] * 2, axis=-1)
    cexp = lambda c: jnp.repeat(c.transpose(0, 2, 1), t, axis=-1)
    grp = lambda *s: pl.BlockSpec((1,) + s, lambda i: (i,) + (0,) * len(s))
    return pl.pallas_call(
        _ssm_prep_kernel,
        grid=(g,),
        in_specs=[grp(p, 1)] * 3 + [grp(1, 2 * p)] * 3 + [grp(hh, 2 * p)] * 2 + [grp(p, width)] * 2,
        out_specs=[grp(width + 2 * p, width), grp(width, 2 * p), grp(8, 2 * p)],
        out_shape=[jax.ShapeDtypeStruct((g, width + 2 * p, width), BF16),
                   jax.ShapeDtypeStruct((g, width, 2 * p), BF16),
                   jax.ShapeDtypeStruct((g, 8, 2 * p), F32)],
        compiler_params=_params(),
        name="ssm_prep",
    )(col(ls), col(lam_re), col(lam_im), row2(ls), row2(lam_re), row2(lam_im),
      bt(b_re), bt(b_im), cexp(c_re), cexp(c_im))


def _ssm_main_kernel(chunks_per_seq, u_ref, wbig_ref, w1_ref, lamt_ref, dexp_ref, y_ref):
    p = SSM_STATE
    width = SSM_GROUP * SSM_CHUNK
    u = u_ref[0]
    n = u.shape[0]
    x = _dot(u, w1_ref[0])
    lam = lamt_ref[0]
    ar = lam[0:1, :]
    ai = lam[1:2, :]
    sgn = jnp.where(lax.broadcasted_iota(jnp.int32, (1, 2 * p), 1) < p, -1.0, 1.0)
    c_idx = lax.broadcasted_iota(jnp.int32, (n, 2 * p), 0) % chunks_per_seq
    d = 1
    while d < chunks_per_seq:
        xs = jnp.where(c_idx >= d, pltpu.roll(x, d, 0), 0.0)
        x = x + ar * xs + (sgn * ai) * pltpu.roll(xs, p, 1)
        ar, ai = ar * ar - ai * ai, 2.0 * ar * ai
        d *= 2
    x_prev = jnp.where(c_idx >= 1, pltpu.roll(x, 1, 0), 0.0)
    y = _dot(u, wbig_ref[0, 0:width, :]) + _dot(x_prev.astype(BF16), wbig_ref[0, width:, :])
    y = y + dexp_ref[0] * u.astype(F32)
    y_ref[0] = y.astype(BF16)


def _ssm_main(u_g, wbig, w1, lamt, dexp, chunks_per_seq):
    g, n, width = u_g.shape
    p = SSM_STATE
    grp = lambda *s: pl.BlockSpec((1,) + s, lambda i: (i,) + (0,) * len(s))
    return pl.pallas_call(
        functools.partial(_ssm_main_kernel, chunks_per_seq),
        grid=(g,),
        in_specs=[grp(n, width), grp(width + 2 * p, width), grp(width, 2 * p), grp(8, 2 * p),
                  grp(1, width)],
        out_specs=grp(n, width),
        out_shape=jax.ShapeDtypeStruct((g, n, width), BF16),
        compiler_params=_params(),
        name="ssm_main",
    )(u_g, wbig, w1, lamt, dexp)


def _merge_kernel(x_ref, ing_ref, inb_ref, ys_ref, ma_ref, gb_ref, wglu_ref, wmix_ref, g_ref, b_ref,
                  o_ref):
    z = _dot(ys_ref[...], wglu_ref[...])
    yb = z[:, :D_MODEL] * jax.nn.sigmoid(z[:, D_MODEL:])
    merged = ma_ref[...].astype(F32) + gb_ref[...].astype(F32) * yb
    mix = _dot(merged.astype(BF16), wmix_ref[...])
    h = _layer_norm(x_ref[...], ing_ref[...], inb_ref[...])
    o_ref[...] = _layer_norm(DEEPNORM_ALPHA * h + mix, g_ref[...], b_ref[...])


def _merge(x2, ing, inb, ys, ma, gb, w_glu, w_mix, g, b):
    n = x2.shape[0]
    row = lambda w: pl.BlockSpec((ROW_TILE, w), lambda i: (i, 0))
    vec = _const_spec((1, D_MODEL))
    return pl.pallas_call(
        _merge_kernel,
        grid=(n // ROW_TILE,),
        in_specs=[row(D_MODEL), vec, vec, row(D_SSM), row(D_MODEL), row(D_MODEL),
                  _const_spec((D_SSM, 2 * D_MODEL)), _const_spec((D_MODEL, D_MODEL)), vec, vec],
        out_specs=row(D_MODEL),
        out_shape=jax.ShapeDtypeStruct((n, D_MODEL), F32),
        compiler_params=_params(),
        name="merge_ln1",
    )(x2, ing, inb, ys, ma, gb, w_glu, w_mix, g, b)


def _kv_kernel(m_ref, w_ref, k_ref, v_ref):
    kv = _dot(m_ref[...].astype(BF16), w_ref[...])
    k_ref[...] = kv[:, :D_MODEL].astype(BF16)
    v_ref[...] = kv[:, D_MODEL:].astype(BF16)


def _kv(mem2, w_kv, n_mem):
    n = mem2.shape[0]
    row = lambda w: pl.BlockSpec((n_mem, w), lambda i: (i, 0))
    return pl.pallas_call(
        _kv_kernel,
        grid=(n // n_mem,),
        in_specs=[row(D_MODEL), _const_spec((D_MODEL, 2 * D_MODEL))],
        out_specs=[row(D_MODEL), row(D_MODEL)],
        out_shape=[jax.ShapeDtypeStruct((n, D_MODEL), BF16)] * 2,
        compiler_params=_params(),
        name="mem_kv",
    )(mem2, w_kv)


def _attn_kernel(h_ref, k_ref, v_ref, wq_ref, wo_ref, g_ref, b_ref, o_ref):
    h = h_ref[...]
    q = _dot(h.astype(BF16), wq_ref[...]).astype(BF16)
    heads = []
    for hd in range(XA_HEADS):
        sl = slice(hd * XA_HEAD_DIM, (hd + 1) * XA_HEAD_DIM)
        s = lax.dot_general(q[:, sl], k_ref[:, sl], (((1,), (1,)), ((), ())),
                            preferred_element_type=F32) * (XA_HEAD_DIM ** -0.5)
        e = jnp.exp(s - jnp.max(s, axis=-1, keepdims=True))
        pr = e / jnp.sum(e, axis=-1, keepdims=True)
        heads.append(_dot(pr.astype(BF16), v_ref[:, sl]).astype(BF16))
    xa = _dot(jnp.concatenate(heads, axis=1), wo_ref[...])
    o_ref[...] = _layer_norm(DEEPNORM_ALPHA * h + xa, g_ref[...], b_ref[...])


def _attn(h1, k, v, wq, wo, g, b, seq, n_mem):
    n = h1.shape[0]
    tiles_per_seq = seq // ROW_TILE
    row = pl.BlockSpec((ROW_TILE, D_MODEL), lambda i: (i, 0))
    memb = pl.BlockSpec((n_mem, D_MODEL), lambda i: (i // tiles_per_seq, 0))
    vec = _const_spec((1, D_MODEL))
    sq = _const_spec((D_MODEL, D_MODEL))
    return pl.pallas_call(
        _attn_kernel,
        grid=(n // ROW_TILE,),
        in_specs=[row, memb, memb, sq, sq, vec, vec],
        out_specs=row,
        out_shape=jax.ShapeDtypeStruct((n, D_MODEL), F32),
        compiler_params=_params(),
        name="mem_attn_ln2",
    )(h1, k, v, wq, wo, g, b)


def _mlp_kernel(h_ref, wu_ref, wd_ref, g_ref, b_ref, o_ref):
    h = h_ref[...]
    hb = h.astype(BF16)
    ff = jnp.zeros((ROW_TILE, D_MODEL), F32)
    for c in range(D_FF // D_MODEL):
        sl = slice(c * D_MODEL, (c + 1) * D_MODEL)
        z = jnp.maximum(_dot(hb, wu_ref[:, sl]), 0.0)
        ff = ff + _dot((z * z).astype(BF16), wd_ref[sl, :])
    o_ref[...] = _layer_norm(DEEPNORM_ALPHA * h + ff, g_ref[...], b_ref[...])


def _mlp(h2, w_up, w_down, g, b):
    n = h2.shape[0]
    row = pl.BlockSpec((ROW_TILE, D_MODEL), lambda i: (i, 0))
    vec = _const_spec((1, D_MODEL))
    return pl.pallas_call(
        _mlp_kernel,
        grid=(n // ROW_TILE,),
        in_specs=[row, _const_spec((D_MODEL, D_FF)), _const_spec((D_FF, D_MODEL)), vec, vec],
        out_specs=row,
        out_shape=jax.ShapeDtypeStruct((n, D_MODEL), F32),
        compiler_params=_params(),
        name="mlp_ln3",
    )(h2, w_up, w_down, g, b)


def _layer(h_in, mem2, ing, inb, w_in, conv_dw, conv_db, conv_ng, conv_nb, w_conv_out, log_step,
           lam_re, lam_im, b_re, b_im, c_re, c_im, d, w_ssm_glu, w_mix_out, ln1_g, ln1_b,
           wq, wkv, wo, ln2_g, ln2_b, w_up, w_down, ln3_g, ln3_b, seq, n_mem):
    n = h_in.shape[0]
    t = SSM_CHUNK
    vec = lambda v: v.reshape(1, -1)
    u, s_in, ga, gb = _in_proj(h_in, ing, inb, w_in.astype(BF16))
    ma = _conv_branch(u, conv_dw, vec(conv_db), vec(conv_ng), vec(conv_nb), w_conv_out.astype(BF16),
                      ga, seq)
    n_chunks = n // t
    u_g = s_in.reshape(n_chunks, t, SSM_GROUPS, SSM_GROUP).transpose(2, 0, 3, 1)
    u_g = u_g.reshape(SSM_GROUPS, n_chunks, SSM_GROUP * t)
    wbig, w1, lamt = _ssm_prep(log_step, lam_re, lam_im, b_re, b_im, c_re, c_im)
    dexp = jnp.repeat(d.reshape(SSM_GROUPS, SSM_GROUP), t, axis=-1)[:, None, :]
    y_g = _ssm_main(u_g, wbig, w1, lamt, dexp, seq // t)
    ys = y_g.reshape(SSM_GROUPS, n_chunks, SSM_GROUP, t).transpose(1, 3, 0, 2).reshape(n, D_SSM)
    h1 = _merge(h_in, ing, inb, ys, ma, gb, w_ssm_glu.astype(BF16), w_mix_out.astype(BF16),
                vec(ln1_g), vec(ln1_b))
    k, v = _kv(mem2, wkv.astype(BF16), n_mem)
    h2 = _attn(h1, k, v, wq.astype(BF16), wo.astype(BF16), vec(ln2_g), vec(ln2_b), seq, n_mem)
    return _mlp(h2, w_up.astype(BF16), w_down.astype(BF16), vec(ln3_g), vec(ln3_b))


def kernel(x, mem, in_norm_g, in_norm_b, w_in, conv_dw, conv_db, conv_norm_g, conv_norm_b, w_conv_out, ssm_log_step, ssm_lambda_re, ssm_lambda_im, ssm_b_re, ssm_b_im, ssm_c_re, ssm_c_im, ssm_d, w_ssm_glu, w_mix_out, ln1_g, ln1_b, xa_wq, xa_wkv, xa_wo, ln2_g, ln2_b, mlp_w_up, mlp_w_down, ln3_g, ln3_b):
    bsz, seq, dm = x.shape
    n_mem = mem.shape[1]
    depth = w_in.shape[0]
    assert depth == 1, "the input normalisation is fused into the single layer's kernels"
    assert dm == D_MODEL and seq % ROW_TILE == 0 and seq % CONV_TILE == 0 and seq % SSM_CHUNK == 0
    x2 = x.reshape(bsz * seq, dm)
    mem2 = mem.reshape(bsz * n_mem, dm)
    out = _layer(x2, mem2, in_norm_g.reshape(1, -1), in_norm_b.reshape(1, -1), w_in[0], conv_dw[0],
                 conv_db[0], conv_norm_g[0], conv_norm_b[0], w_conv_out[0], ssm_log_step[0],
                 ssm_lambda_re[0], ssm_lambda_im[0], ssm_b_re[0], ssm_b_im[0], ssm_c_re[0],
                 ssm_c_im[0], ssm_d[0], w_ssm_glu[0], w_mix_out[0], ln1_g[0], ln1_b[0], xa_wq[0],
                 xa_wkv[0], xa_wo[0], ln2_g[0], ln2_b[0], mlp_w_up[0], mlp_w_down[0], ln3_g[0],
                 ln3_b[0], seq, n_mem)
    return out.reshape(bsz, seq, dm)
```

```python
import functools

import jax
import jax.numpy as jnp
from jax import lax
from jax.experimental import pallas as pl
from jax.experimental.pallas import tpu as pltpu

F32 = jnp.float32
BF16 = jnp.bfloat16

D_MODEL = 1024
D_CONV = 1024
CONV_K = 31
D_SSM = 512
SSM_GROUP = 16
SSM_GROUPS = 32
SSM_STATE = 64
XA_HEADS = 4
XA_HEAD_DIM = 256
D_FF = 4096
LN_EPS = 1e-5
DEEPNORM_ALPHA = 2.0 ** 0.25

LANES = 128
SUBLANES = 8
MXU_WIDTH = 256
SSM_CHUNK = 64
CONV_HALO = 32
CONV_ROWS = 128
ROW_TILE = 512
VMEM_LIMIT = 56 * 1024 * 1024


def _layer_norm(x, g, b):
    mu = jnp.mean(x, axis=-1, keepdims=True)
    xc = x - mu
    var = jnp.mean(xc * xc, axis=-1, keepdims=True)
    return xc * lax.rsqrt(var + LN_EPS) * g + b


def _dot(a, b):
    return jnp.dot(a, b, preferred_element_type=F32)


def _params(semantics="parallel"):
    return pltpu.CompilerParams(dimension_semantics=(semantics,), vmem_limit_bytes=VMEM_LIMIT)


def _const_spec(shape):
    nd = len(shape)
    return pl.BlockSpec(shape, lambda *_: (0,) * nd)


def _in_conv_kernel(tiles_per_seq, x_ref, ing_ref, inb_ref, w_ref, dw_ref, db_ref, ng_ref, nb_ref,
                    wco_ref, ma_ref, ug_ref, gb_ref, ext_ref, s_ref, ga_ref):
    tm = ROW_TILE
    n_col = D_CONV // LANES
    o0, o1, o2, o3 = D_CONV, 2 * D_CONV, 2 * D_CONV + D_SSM, 2 * D_CONV + D_SSM + D_MODEL
    first = (pl.program_id(0) % tiles_per_seq) == 0

    @pl.when(first)
    def _():
        ext_ref[:, 0:CONV_HALO, :] = jnp.zeros((n_col, CONV_HALO, LANES), F32)

    @pl.when(jnp.logical_not(first))
    def _():
        ext_ref[:, 0:CONV_HALO, :] = ext_ref[:, tm:tm + CONV_HALO, :]

    h = _layer_norm(x_ref[...], ing_ref[...], inb_ref[...]).astype(BF16)
    u = _dot(h, w_ref[:, 0:o0]) * jax.nn.sigmoid(_dot(h, w_ref[:, o0:o1]))
    for c in range(n_col):
        ext_ref[c, CONV_HALO:, :] = u[:, c * LANES:(c + 1) * LANES]

    def proj_s():
        s = _dot(h, w_ref[:, o1:o2])
        for q in range(D_SSM // LANES):
            s_ref[q] = s[:, q * LANES:(q + 1) * LANES]

    def proj_gate(dst_ref, col0, piece):
        def run():
            ps = slice(piece * MXU_WIDTH, (piece + 1) * MXU_WIDTH)
            cols_w = slice(col0 + piece * MXU_WIDTH, col0 + (piece + 1) * MXU_WIDTH)
            dst_ref[:, ps] = jax.nn.sigmoid(_dot(h, w_ref[:, cols_w])).astype(BF16)
        return run

    side_work = [proj_s]
    for piece in range(D_MODEL // MXU_WIDTH):
        side_work += [proj_gate(ga_ref, o2, piece), proj_gate(gb_ref, o3, piece)]

    base = CONV_HALO - (CONV_K - 1)
    cols = []
    for c in range(n_col):
        cs = slice(c * LANES, (c + 1) * LANES)
        blocks = []
        for r0 in range(0, tm, CONV_ROWS):
            acc = jnp.broadcast_to(db_ref[:, cs], (CONV_ROWS, LANES))
            for j in range(CONV_K):
                acc = acc + dw_ref[j:j + 1, cs] * ext_ref[c, base + j + r0:base + j + r0 + CONV_ROWS, :]
            blocks.append(acc)
        cols.append(jnp.concatenate(blocks, axis=0))
        lo = (c * len(side_work)) // n_col
        hi = ((c + 1) * len(side_work)) // n_col
        for work in side_work[lo:hi]:
            work()
    conv = _layer_norm(jnp.concatenate(cols, axis=1), ng_ref[...], nb_ref[...])
    conv = conv * jax.nn.sigmoid(conv)
    ya = _dot(conv.astype(BF16), wco_ref[...])
    ma_ref[...] = (ga_ref[...].astype(F32) * ya).astype(BF16)

    t = SSM_CHUNK
    per_tile = LANES // SSM_GROUP
    seg = lax.broadcasted_iota(jnp.int32, (SUBLANES, LANES), 1) // SSM_GROUP
    for m in range(t // per_tile):
        for q in range(D_SSM // LANES):
            tiles = [s_ref[q, pl.ds(per_tile * m + k0, tm // t, stride=t), :] for k0 in range(per_tile)]
            for gam in range(per_tile):
                out = None
                for k0 in range(per_tile):
                    shift = (SSM_GROUP * (k0 - gam)) % LANES
                    piece = tiles[k0] if shift == 0 else pltpu.roll(tiles[k0], shift, 1)
                    out = piece if out is None else jnp.where(seg == k0, piece, out)
                ug_ref[q * per_tile + gam, :, m * LANES:(m + 1) * LANES] = out


def _in_conv(x2, ing, inb, w_in, dw, db, ng, nb, w_conv_out, seq):
    n = x2.shape[0]
    tm = ROW_TILE
    d_in = w_in.shape[1]
    chunks = tm // SSM_CHUNK
    assert chunks == SUBLANES, "one row tile must fill the sublanes of a regrouped S5 block"
    row = lambda w: pl.BlockSpec((tm, w), lambda i: (i, 0))
    vec = _const_spec((1, D_MODEL))
    return pl.pallas_call(
        functools.partial(_in_conv_kernel, seq // tm),
        grid=(n // tm,),
        in_specs=[row(D_MODEL), vec, vec, _const_spec((D_MODEL, d_in)), _const_spec((CONV_K, D_CONV)),
                  vec, vec, vec, _const_spec((D_CONV, D_MODEL))],
        out_specs=[row(D_MODEL),
                   pl.BlockSpec((SSM_GROUPS, chunks, SSM_GROUP * SSM_CHUNK), lambda i: (0, i, 0)),
                   row(D_MODEL)],
        out_shape=[jax.ShapeDtypeStruct((n, D_MODEL), BF16),
                   jax.ShapeDtypeStruct((SSM_GROUPS, n // SSM_CHUNK, SSM_GROUP * SSM_CHUNK), F32),
                   jax.ShapeDtypeStruct((n, D_MODEL), BF16)],
        scratch_shapes=[pltpu.VMEM((D_CONV // LANES, CONV_HALO + tm, LANES), F32),
                        pltpu.VMEM((D_SSM // LANES, tm, LANES), F32),
                        pltpu.VMEM((tm, D_MODEL), BF16)],
        compiler_params=_params("arbitrary"),
        name="in_conv",
    )(x2, ing, inb, w_in, dw, db, ng, nb, w_conv_out)


def _ssm_prep_kernel(lsc_ref, lrc_ref, lic_ref, lsr_ref, lrr_ref, lir_ref, btr_ref, bti_ref,
                     cer_ref, cei_ref, wbig_ref, w1_ref, lamt_ref):
    t = SSM_CHUNK
    p = SSM_STATE
    hh = SSM_GROUP
    width = hh * t
    a_c = lrc_ref[0] * jnp.exp(lsc_ref[0])
    th_c = lic_ref[0] * jnp.exp(lsc_ref[0])
    hp = lax.Precision.HIGHEST
    j = (lax.broadcasted_iota(jnp.int32, (p, LANES), 1) % t).astype(F32)
    mag = jnp.exp(a_c * j)
    l0r = mag * jnp.cos(th_c * j)
    l0i = mag * jnp.sin(th_c * j)
    onehot = (lax.broadcasted_iota(jnp.int32, (t, width), 0)
              == lax.broadcasted_iota(jnp.int32, (t, width), 1) // hh).astype(F32)
    l0r = jnp.dot(l0r[:, :t], onehot, precision=hp, preferred_element_type=F32)
    l0i = jnp.dot(l0i[:, :t], onehot, precision=hp, preferred_element_type=F32)
    cer = cer_ref[0]
    cei = cei_ref[0]
    cl0r = cer * l0r - cei * l0i
    cl0i = cer * l0i + cei * l0r
    lam_r = jnp.exp(a_c) * jnp.cos(th_c)
    lam_i = jnp.exp(a_c) * jnp.sin(th_c)
    cl1r = cl0r * lam_r - cl0i * lam_i
    cl1i = cl0r * lam_i + cl0i * lam_r
    wbig_ref[0, width:width + p, :] = cl1r.astype(BF16)
    wbig_ref[0, width + p:width + 2 * p, :] = (-cl1i).astype(BF16)

    step_r = jnp.exp(lsr_ref[0])
    lr = lrr_ref[0]
    li = lir_ref[0]
    a_r = lr * step_r
    th_r = li * step_r
    ar = jnp.exp(a_r) * jnp.cos(th_r)
    ai = jnp.exp(a_r) * jnp.sin(th_r)
    den = lr * lr + li * li
    nr = ar - 1.0
    cr = (nr * lr + ai * li) / den
    ci = (ai * lr - nr * li) / den
    btr = btr_ref[0]
    bti = bti_ref[0]
    bbr = cr * btr - ci * bti
    bbi = cr * bti + ci * btr

    g2 = (jnp.dot(bbr[:, :p], cl0r, precision=hp, preferred_element_type=F32)
          - jnp.dot(bbi[:, :p], cl0i, precision=hp, preferred_element_type=F32))
    col = lax.broadcasted_iota(jnp.int32, (hh, width), 1)
    for k in range(t):
        taps = g2 if k == 0 else pltpu.roll(g2, k * hh, 1)
        wbig_ref[0, k * hh:(k + 1) * hh, :] = jnp.where(col >= k * hh, taps, 0.0).astype(BF16)

    kk = ((t - 1) - lax.broadcasted_iota(jnp.int32, (t, 2 * p), 0)).astype(F32)
    magk = jnp.exp(a_r * kk)
    lkr = magk * jnp.cos(th_r * kk)
    lki = magk * jnp.sin(th_r * kk)
    real_half = lax.broadcasted_iota(jnp.int32, (hh, 2 * p), 1) < p
    for k in range(t):
        er = lkr[k:k + 1, :]
        ei = lki[k:k + 1, :]
        w1_ref[0, k * hh:(k + 1) * hh, :] = jnp.where(real_half, er * bbr - ei * bbi,
                                                      er * bbi + ei * bbr).astype(BF16)

    magt = jnp.exp(a_r * float(t))
    ltr = magt * jnp.cos(th_r * float(t))
    lti = magt * jnp.sin(th_r * float(t))
    sub = lax.broadcasted_iota(jnp.int32, (SUBLANES, 2 * p), 0)
    lamt_ref[0] = jnp.where(sub == 0, ltr, jnp.where(sub == 1, lti, 0.0))


def _ssm_prep(log_step, lam_re, lam_im, b_re, b_im, c_re, c_im):
    g, p, hh, t = SSM_GROUPS, SSM_STATE, SSM_GROUP, SSM_CHUNK
    width = hh * t
    col = lambda v: v[:, :, None]
    row2 = lambda v: jnp.concatenate([v, v], axis=-1)[:, None, :]
    ls = jnp.broadcast_to(log_step[:, None], (g, p))
    bt = lambda b: jnp.concatenate([b.transpose(0, 2, 1)] * 2, axis=-1)
    cexp = lambda c: jnp.tile(c.transpose(0, 2, 1), (1, 1, t))
    grp = lambda *s: pl.BlockSpec((1,) + s, lambda i: (i,) + (0,) * len(s))
    return pl.pallas_call(
        _ssm_prep_kernel,
        grid=(g,),
        in_specs=[grp(p, 1)] * 3 + [grp(1, 2 * p)] * 3 + [grp(hh, 2 * p)] * 2 + [grp(p, width)] * 2,
        out_specs=[grp(width + 2 * p, width), grp(width, 2 * p), grp(SUBLANES, 2 * p)],
        out_shape=[jax.ShapeDtypeStruct((g, width + 2 * p, width), BF16),
                   jax.ShapeDtypeStruct((g, width, 2 * p), BF16),
                   jax.ShapeDtypeStruct((g, SUBLANES, 2 * p), F32)],
        compiler_params=_params(),
        name="ssm_prep",
    )(col(ls), col(lam_re), col(lam_im), row2(ls), row2(lam_re), row2(lam_im),
      bt(b_re), bt(b_im), cexp(c_re), cexp(c_im))


def _ssm_main_kernel(chunks_per_seq, u_ref, wbig_ref, w1_ref, lamt_ref, dexp_ref, y_ref):
    p = SSM_STATE
    width = SSM_GROUP * SSM_CHUNK
    uf = u_ref[0]
    u = uf.astype(BF16)
    n = u.shape[0]
    x = _dot(u, w1_ref[0])
    lam = lamt_ref[0]
    ar = lam[0:1, :]
    ai = lam[1:2, :]
    sgn = jnp.where(lax.broadcasted_iota(jnp.int32, (1, 2 * p), 1) < p, -1.0, 1.0)
    c_idx = lax.broadcasted_iota(jnp.int32, (n, 2 * p), 0) % chunks_per_seq
    d = 1
    while d < chunks_per_seq:
        xs = jnp.where(c_idx >= d, pltpu.roll(x, d, 0), 0.0)
        x = x + ar * xs + (sgn * ai) * pltpu.roll(xs, p, 1)
        ar, ai = ar * ar - ai * ai, 2.0 * ar * ai
        d *= 2
    x_prev = jnp.where(c_idx >= 1, pltpu.roll(x, 1, 0), 0.0)
    y = _dot(u, wbig_ref[0, 0:width, :]) + _dot(x_prev.astype(BF16), wbig_ref[0, width:, :])
    y_ref[0] = (y + dexp_ref[0] * uf).astype(BF16)


def _ssm_main(u_g, wbig, w1, lamt, dexp, chunks_per_seq):
    g, n, width = u_g.shape
    p = SSM_STATE
    grp = lambda *s: pl.BlockSpec((1,) + s, lambda i: (i,) + (0,) * len(s))
    return pl.pallas_call(
        functools.partial(_ssm_main_kernel, chunks_per_seq),
        grid=(g,),
        in_specs=[grp(n, width), grp(width + 2 * p, width), grp(width, 2 * p), grp(SUBLANES, 2 * p),
                  grp(1, width)],
        out_specs=grp(n, width),
        out_shape=jax.ShapeDtypeStruct((g, n, width), BF16),
        compiler_params=_params(),
        name="ssm_main",
    )(u_g, wbig, w1, lamt, dexp)


def _merge_kernel(x_ref, ing_ref, inb_ref, ys_ref, ma_ref, gb_ref, wglu_ref, wmix_ref, g_ref, b_ref,
                  o_ref):
    z = _dot(ys_ref[...], wglu_ref[...])
    yb = z[:, :D_MODEL] * jax.nn.sigmoid(z[:, D_MODEL:])
    merged = ma_ref[...].astype(F32) + gb_ref[...].astype(F32) * yb
    mix = _dot(merged.astype(BF16), wmix_ref[...])
    h = _layer_norm(x_ref[...], ing_ref[...], inb_ref[...])
    o_ref[...] = _layer_norm(DEEPNORM_ALPHA * h + mix, g_ref[...], b_ref[...])


def _merge(x2, ing, inb, ys, ma, gb, w_glu, w_mix, g, b):
    n = x2.shape[0]
    row = lambda w: pl.BlockSpec((ROW_TILE, w), lambda i: (i, 0))
    vec = _const_spec((1, D_MODEL))
    return pl.pallas_call(
        _merge_kernel,
        grid=(n // ROW_TILE,),
        in_specs=[row(D_MODEL), vec, vec, row(D_SSM), row(D_MODEL), row(D_MODEL),
                  _const_spec((D_SSM, 2 * D_MODEL)), _const_spec((D_MODEL, D_MODEL)), vec, vec],
        out_specs=row(D_MODEL),
        out_shape=jax.ShapeDtypeStruct((n, D_MODEL), F32),
        compiler_params=_params(),
        name="merge_ln1",
    )(x2, ing, inb, ys, ma, gb, w_glu, w_mix, g, b)


def _kv_kernel(m_ref, w_ref, k_ref, v_ref):
    kv = _dot(m_ref[...].astype(BF16), w_ref[...])
    k_ref[...] = kv[:, :D_MODEL].astype(BF16)
    v_ref[...] = kv[:, D_MODEL:].astype(BF16)


def _kv(mem2, w_kv, n_mem):
    n = mem2.shape[0]
    row = lambda w: pl.BlockSpec((n_mem, w), lambda i: (i, 0))
    return pl.pallas_call(
        _kv_kernel,
        grid=(n // n_mem,),
        in_specs=[row(D_MODEL), _const_spec((D_MODEL, 2 * D_MODEL))],
        out_specs=[row(D_MODEL), row(D_MODEL)],
        out_shape=[jax.ShapeDtypeStruct((n, D_MODEL), BF16)] * 2,
        compiler_params=_params(),
        name="mem_kv",
    )(mem2, w_kv)


def _attn_kernel(h_ref, k_ref, v_ref, wq_ref, wo_ref, g_ref, b_ref, o_ref):
    h = h_ref[...]
    q = _dot(h.astype(BF16), wq_ref[...]).astype(BF16)
    heads = []
    for hd in range(XA_HEADS):
        sl = slice(hd * XA_HEAD_DIM, (hd + 1) * XA_HEAD_DIM)
        s = lax.dot_general(q[:, sl], k_ref[:, sl], (((1,), (1,)), ((), ())),
                            preferred_element_type=F32) * (XA_HEAD_DIM ** -0.5)
        e = jnp.exp(s - jnp.max(s, axis=-1, keepdims=True))
        pr = e / jnp.sum(e, axis=-1, keepdims=True)
        heads.append(_dot(pr.astype(BF16), v_ref[:, sl]).astype(BF16))
    xa = _dot(jnp.concatenate(heads, axis=1), wo_ref[...])
    o_ref[...] = _layer_norm(DEEPNORM_ALPHA * h + xa, g_ref[...], b_ref[...])


def _attn(h1, k, v, wq, wo, g, b, seq, n_mem):
    n = h1.shape[0]
    tiles_per_seq = seq // ROW_TILE
    row = pl.BlockSpec((ROW_TILE, D_MODEL), lambda i: (i, 0))
    memb = pl.BlockSpec((n_mem, D_MODEL), lambda i: (i // tiles_per_seq, 0))
    vec = _const_spec((1, D_MODEL))
    sq = _const_spec((D_MODEL, D_MODEL))
    return pl.pallas_call(
        _attn_kernel,
        grid=(n // ROW_TILE,),
        in_specs=[row, memb, memb, sq, sq, vec, vec],
        out_specs=row,
        out_shape=jax.ShapeDtypeStruct((n, D_MODEL), F32),
        compiler_params=_params(),
        name="mem_attn_ln2",
    )(h1, k, v, wq, wo, g, b)


def _mlp_kernel(h_ref, wu_ref, wd_ref, g_ref, b_ref, o_ref):
    h = h_ref[...]
    hb = h.astype(BF16)
    ff = jnp.zeros((ROW_TILE, D_MODEL), F32)
    for c in range(D_FF // D_MODEL):
        sl = slice(c * D_MODEL, (c + 1) * D_MODEL)
        z = jnp.maximum(_dot(hb, wu_ref[:, sl]), 0.0)
        ff = ff + _dot((z * z).astype(BF16), wd_ref[sl, :])
    o_ref[...] = _layer_norm(DEEPNORM_ALPHA * h + ff, g_ref[...], b_ref[...])


def _mlp(h2, w_up, w_down, g, b):
    n = h2.shape[0]
    row = pl.BlockSpec((ROW_TILE, D_MODEL), lambda i: (i, 0))
    vec = _const_spec((1, D_MODEL))
    return pl.pallas_call(
        _mlp_kernel,
        grid=(n // ROW_TILE,),
        in_specs=[row, _const_spec((D_MODEL, D_FF)), _const_spec((D_FF, D_MODEL)), vec, vec],
        out_specs=row,
        out_shape=jax.ShapeDtypeStruct((n, D_MODEL), F32),
        compiler_params=_params(),
        name="mlp_ln3",
    )(h2, w_up, w_down, g, b)


def _layer(h_in, mem2, ing, inb, w_in, conv_dw, conv_db, conv_ng, conv_nb, w_conv_out, log_step,
           lam_re, lam_im, b_re, b_im, c_re, c_im, d, w_ssm_glu, w_mix_out, ln1_g, ln1_b,
           wq, wkv, wo, ln2_g, ln2_b, w_up, w_down, ln3_g, ln3_b, seq, n_mem):
    n = h_in.shape[0]
    t = SSM_CHUNK
    vec = lambda v: v.reshape(1, -1)
    ma, u_g, gb = _in_conv(h_in, ing, inb, w_in.astype(BF16), conv_dw, vec(conv_db), vec(conv_ng),
                           vec(conv_nb), w_conv_out.astype(BF16), seq)
    wbig, w1, lamt = _ssm_prep(log_step, lam_re, lam_im, b_re, b_im, c_re, c_im)
    dexp = jnp.tile(d.reshape(SSM_GROUPS, SSM_GROUP), (1, t))[:, None, :]
    y_g = _ssm_main(u_g, wbig, w1, lamt, dexp, seq // t)
    n_chunks = n // t
    ys = y_g.reshape(SSM_GROUPS, n_chunks, t, SSM_GROUP).transpose(1, 2, 0, 3).reshape(n, D_SSM)
    h1 = _merge(h_in, ing, inb, ys, ma, gb, w_ssm_glu.astype(BF16), w_mix_out.astype(BF16),
                vec(ln1_g), vec(ln1_b))
    k, v = _kv(mem2, wkv.astype(BF16), n_mem)
    h2 = _attn(h1, k, v, wq.astype(BF16), wo.astype(BF16), vec(ln2_g), vec(ln2_b), seq, n_mem)
    return _mlp(h2, w_up.astype(BF16), w_down.astype(BF16), vec(ln3_g), vec(ln3_b))


def kernel(x, mem, in_norm_g, in_norm_b, w_in, conv_dw, conv_db, conv_norm_g, conv_norm_b, w_conv_out, ssm_log_step, ssm_lambda_re, ssm_lambda_im, ssm_b_re, ssm_b_im, ssm_c_re, ssm_c_im, ssm_d, w_ssm_glu, w_mix_out, ln1_g, ln1_b, xa_wq, xa_wkv, xa_wo, ln2_g, ln2_b, mlp_w_up, mlp_w_down, ln3_g, ln3_b):
    bsz, seq, dm = x.shape
    n_mem = mem.shape[1]
    depth = w_in.shape[0]
    assert depth == 1, "the input normalisation is fused into the single layer's kernels"
    assert dm == D_MODEL and seq % ROW_TILE == 0 and seq % SSM_CHUNK == 0
    x2 = x.reshape(bsz * seq, dm)
    mem2 = mem.reshape(bsz * n_mem, dm)
    out = _layer(x2, mem2, in_norm_g.reshape(1, -1), in_norm_b.reshape(1, -1), w_in[0], conv_dw[0],
                 conv_db[0], conv_norm_g[0], conv_norm_b[0], w_conv_out[0], ssm_log_step[0],
                 ssm_lambda_re[0], ssm_lambda_im[0], ssm_b_re[0], ssm_b_im[0], ssm_c_re[0],
                 ssm_c_im[0], ssm_d[0], w_ssm_glu[0], w_mix_out[0], ln1_g[0], ln1_b[0], xa_wq[0],
                 xa_wkv[0], xa_wo[0], ln2_g[0], ln2_b[0], mlp_w_up[0], mlp_w_down[0], ln3_g[0],
                 ln3_b[0], seq, n_mem)
    return out.reshape(bsz, seq, dm)
```

```python
import functools

import jax
import jax.numpy as jnp
from jax import lax
from jax.experimental import pallas as pl
from jax.experimental.pallas import tpu as pltpu

F32 = jnp.float32
BF16 = jnp.bfloat16

D_MODEL = 1024
D_CONV = 1024
CONV_K = 31
D_SSM = 512
SSM_GROUP = 16
SSM_GROUPS = 32
SSM_STATE = 64
XA_HEADS = 4
XA_HEAD_DIM = 256
D_FF = 4096
LN_EPS = 1e-5
DEEPNORM_ALPHA = 2.0 ** 0.25

LANES = 128
SUBLANES = 8
MXU_WIDTH = 256
SSM_CHUNK = 64
CONV_HALO = 32
CONV_ROWS = 128
ROW_TILE = 512
VMEM_LIMIT = 56 * 1024 * 1024


def _layer_norm(x, g, b):
    mu = jnp.mean(x, axis=-1, keepdims=True)
    xc = x - mu
    var = jnp.mean(xc * xc, axis=-1, keepdims=True)
    return xc * lax.rsqrt(var + LN_EPS) * g + b


def _dot(a, b):
    return jnp.dot(a, b, preferred_element_type=F32)


def _params(semantics="parallel"):
    return pltpu.CompilerParams(dimension_semantics=(semantics,), vmem_limit_bytes=VMEM_LIMIT)


def _const_spec(shape):
    nd = len(shape)
    return pl.BlockSpec(shape, lambda *_: (0,) * nd)


def _block_swap_matrix():
    idx = jnp.arange(SUBLANES * LANES)
    a, b, h = idx // LANES, (idx % LANES) // SSM_GROUP, idx % SSM_GROUP
    return (idx[None, :] == (b * LANES + a * SSM_GROUP + h)[:, None]).astype(BF16)


def _chunk_transpose_matrix(rows, inner):
    idx = jnp.arange(rows)
    src = (idx % (rows // inner)) * inner + idx // (rows // inner)
    return (idx[None, :] == src[:, None]).astype(BF16)


def _in_conv_kernel(tiles_per_seq, x_ref, ing_ref, inb_ref, w_ref, rowperm_ref, perm_ref, dw_ref,
                    db_ref, ng_ref, nb_ref, wco_ref, ma_ref, ug_ref, gb_ref, ext_ref, raw_ref):
    tm = ROW_TILE
    n_col = D_CONV // LANES
    o0, o1, o2, o3 = D_CONV, 2 * D_CONV, 2 * D_CONV + D_SSM, 2 * D_CONV + D_SSM + D_MODEL
    first = (pl.program_id(0) % tiles_per_seq) == 0

    @pl.when(first)
    def _():
        ext_ref[:, 0:CONV_HALO, :] = jnp.zeros((n_col, CONV_HALO, LANES), F32)

    @pl.when(jnp.logical_not(first))
    def _():
        ext_ref[:, 0:CONV_HALO, :] = ext_ref[:, tm:tm + CONV_HALO, :]

    h = _layer_norm(x_ref[...], ing_ref[...], inb_ref[...]).astype(BF16)
    u = _dot(h, w_ref[:, 0:o0]) * jax.nn.sigmoid(_dot(h, w_ref[:, o0:o1]))
    for c in range(n_col):
        ext_ref[c, CONV_HALO:, :] = u[:, c * LANES:(c + 1) * LANES]

    raw_ref[...] = _dot(h, w_ref[:, o2:])

    t = SSM_CHUNK
    per_tile = LANES // SSM_GROUP
    n_q = D_SSM // LANES
    n_c = tm // t
    s_kc = _dot(rowperm_ref[...], _dot(h, w_ref[:, o1:o2]).astype(BF16))
    rows = []
    for m in range(t // per_tile):
        for q in range(n_q):
            rows.append(jnp.concatenate(
                [s_kc[(per_tile * m + k0) * n_c:(per_tile * m + k0 + 1) * n_c, q * LANES:(q + 1) * LANES]
                 for k0 in range(per_tile)], axis=1))
    swapped = _dot(jnp.concatenate(rows, axis=0).astype(BF16), perm_ref[...])
    for m in range(t // per_tile):
        for q in range(n_q):
            r0 = (m * n_q + q) * SUBLANES
            for gam in range(per_tile):
                ug_ref[q * per_tile + gam, :, m * LANES:(m + 1) * LANES] = (
                    swapped[r0:r0 + SUBLANES, gam * LANES:(gam + 1) * LANES])

    base = CONV_HALO - (CONV_K - 1)
    cols = []
    for c in range(n_col):
        cs = slice(c * LANES, (c + 1) * LANES)
        blocks = []
        for r0 in range(0, tm, CONV_ROWS):
            acc = jnp.broadcast_to(db_ref[:, cs], (CONV_ROWS, LANES))
            for j in range(CONV_K):
                acc = acc + dw_ref[j:j + 1, cs] * ext_ref[c, base + j + r0:base + j + r0 + CONV_ROWS, :]
            blocks.append(acc)
        cols.append(jnp.concatenate(blocks, axis=0))
    conv = _layer_norm(jnp.concatenate(cols, axis=1), ng_ref[...], nb_ref[...])
    conv = conv * jax.nn.sigmoid(conv)
    ya = _dot(conv.astype(BF16), wco_ref[...])
    ma_ref[...] = (jax.nn.sigmoid(raw_ref[:, :D_MODEL]) * ya).astype(BF16)
    gb_ref[...] = jax.nn.sigmoid(raw_ref[:, D_MODEL:]).astype(BF16)


def _in_conv(x2, ing, inb, w_in, dw, db, ng, nb, w_conv_out, seq):
    n = x2.shape[0]
    tm = ROW_TILE
    d_in = w_in.shape[1]
    chunks = tm // SSM_CHUNK
    assert chunks == SUBLANES, "one row tile must fill the sublanes of a regrouped S5 block"
    row = lambda w: pl.BlockSpec((tm, w), lambda i: (i, 0))
    vec = _const_spec((1, D_MODEL))
    return pl.pallas_call(
        functools.partial(_in_conv_kernel, seq // tm),
        grid=(n // tm,),
        in_specs=[row(D_MODEL), vec, vec, _const_spec((D_MODEL, d_in)),
                  _const_spec((tm, tm)), _const_spec((SUBLANES * LANES, SUBLANES * LANES)),
                  _const_spec((CONV_K, D_CONV)), vec, vec, vec, _const_spec((D_CONV, D_MODEL))],
        out_specs=[row(D_MODEL),
                   pl.BlockSpec((SSM_GROUPS, chunks, SSM_GROUP * SSM_CHUNK), lambda i: (0, i, 0)),
                   row(D_MODEL)],
        out_shape=[jax.ShapeDtypeStruct((n, D_MODEL), BF16),
                   jax.ShapeDtypeStruct((SSM_GROUPS, n // SSM_CHUNK, SSM_GROUP * SSM_CHUNK), F32),
                   jax.ShapeDtypeStruct((n, D_MODEL), BF16)],
        scratch_shapes=[pltpu.VMEM((D_CONV // LANES, CONV_HALO + tm, LANES), F32),
                        pltpu.VMEM((tm, 2 * D_MODEL), F32)],
        compiler_params=_params("arbitrary"),
        name="in_conv",
    )(x2, ing, inb, w_in, _chunk_transpose_matrix(tm, SSM_CHUNK), _block_swap_matrix(), dw, db, ng, nb,
      w_conv_out)


def _ssm_prep_kernel(lsc_ref, lrc_ref, lic_ref, lsr_ref, lrr_ref, lir_ref, btr_ref, bti_ref,
                     cer_ref, cei_ref, wbig_ref, w1_ref, lamt_ref):
    t = SSM_CHUNK
    p = SSM_STATE
    hh = SSM_GROUP
    width = hh * t
    a_c = lrc_ref[0] * jnp.exp(lsc_ref[0])
    th_c = lic_ref[0] * jnp.exp(lsc_ref[0])
    hp = lax.Precision.HIGHEST
    j = (lax.broadcasted_iota(jnp.int32, (p, LANES), 1) % t).astype(F32)
    mag = jnp.exp(a_c * j)
    l0r = mag * jnp.cos(th_c * j)
    l0i = mag * jnp.sin(th_c * j)
    onehot = (lax.broadcasted_iota(jnp.int32, (t, width), 0)
              == lax.broadcasted_iota(jnp.int32, (t, width), 1) // hh).astype(F32)
    l0r = jnp.dot(l0r[:, :t], onehot, precision=hp, preferred_element_type=F32)
    l0i = jnp.dot(l0i[:, :t], onehot, precision=hp, preferred_element_type=F32)
    cer = cer_ref[0]
    cei = cei_ref[0]
    cl0r = cer * l0r - cei * l0i
    cl0i = cer * l0i + cei * l0r
    lam_r = jnp.exp(a_c) * jnp.cos(th_c)
    lam_i = jnp.exp(a_c) * jnp.sin(th_c)
    cl1r = cl0r * lam_r - cl0i * lam_i
    cl1i = cl0r * lam_i + cl0i * lam_r
    wbig_ref[0, width:width + p, :] = cl1r.astype(BF16)
    wbig_ref[0, width + p:width + 2 * p, :] = (-cl1i).astype(BF16)

    step_r = jnp.exp(lsr_ref[0])
    lr = lrr_ref[0]
    li = lir_ref[0]
    a_r = lr * step_r
    th_r = li * step_r
    ar = jnp.exp(a_r) * jnp.cos(th_r)
    ai = jnp.exp(a_r) * jnp.sin(th_r)
    den = lr * lr + li * li
    nr = ar - 1.0
    cr = (nr * lr + ai * li) / den
    ci = (ai * lr - nr * li) / den
    btr = btr_ref[0]
    bti = bti_ref[0]
    bbr = cr * btr - ci * bti
    bbi = cr * bti + ci * btr

    g2 = (jnp.dot(bbr[:, :p], cl0r, precision=hp, preferred_element_type=F32)
          - jnp.dot(bbi[:, :p], cl0i, precision=hp, preferred_element_type=F32))
    col = lax.broadcasted_iota(jnp.int32, (hh, width), 1)
    for k in range(t):
        taps = g2 if k == 0 else pltpu.roll(g2, k * hh, 1)
        wbig_ref[0, k * hh:(k + 1) * hh, :] = jnp.where(col >= k * hh, taps, 0.0).astype(BF16)

    kk = ((t - 1) - lax.broadcasted_iota(jnp.int32, (t, 2 * p), 0)).astype(F32)
    magk = jnp.exp(a_r * kk)
    lkr = magk * jnp.cos(th_r * kk)
    lki = magk * jnp.sin(th_r * kk)
    real_half = lax.broadcasted_iota(jnp.int32, (hh, 2 * p), 1) < p
    for k in range(t):
        er = lkr[k:k + 1, :]
        ei = lki[k:k + 1, :]
        w1_ref[0, k * hh:(k + 1) * hh, :] = jnp.where(real_half, er * bbr - ei * bbi,
                                                      er * bbi + ei * bbr).astype(BF16)

    magt = jnp.exp(a_r * float(t))
    ltr = magt * jnp.cos(th_r * float(t))
    lti = magt * jnp.sin(th_r * float(t))
    sub = lax.broadcasted_iota(jnp.int32, (SUBLANES, 2 * p), 0)
    lamt_ref[0] = jnp.where(sub == 0, ltr, jnp.where(sub == 1, lti, 0.0))


def _ssm_prep(log_step, lam_re, lam_im, b_re, b_im, c_re, c_im):
    g, p, hh, t = SSM_GROUPS, SSM_STATE, SSM_GROUP, SSM_CHUNK
    width = hh * t
    col = lambda v: v[:, :, None]
    row2 = lambda v: jnp.concatenate([v, v], axis=-1)[:, None, :]
    ls = jnp.broadcast_to(log_step[:, None], (g, p))
    bt = lambda b: jnp.concatenate([b.transpose(0, 2, 1)] * 2, axis=-1)
    cexp = lambda c: jnp.tile(c.transpose(0, 2, 1), (1, 1, t))
    grp = lambda *s: pl.BlockSpec((1,) + s, lambda i: (i,) + (0,) * len(s))
    return pl.pallas_call(
        _ssm_prep_kernel,
        grid=(g,),
        in_specs=[grp(p, 1)] * 3 + [grp(1, 2 * p)] * 3 + [grp(hh, 2 * p)] * 2 + [grp(p, width)] * 2,
        out_specs=[grp(width + 2 * p, width), grp(width, 2 * p), grp(SUBLANES, 2 * p)],
        out_shape=[jax.ShapeDtypeStruct((g, width + 2 * p, width), BF16),
                   jax.ShapeDtypeStruct((g, width, 2 * p), BF16),
                   jax.ShapeDtypeStruct((g, SUBLANES, 2 * p), F32)],
        compiler_params=_params(),
        name="ssm_prep",
    )(col(ls), col(lam_re), col(lam_im), row2(ls), row2(lam_re), row2(lam_im),
      bt(b_re), bt(b_im), cexp(c_re), cexp(c_im))


def _ssm_main_kernel(chunks_per_seq, u_ref, wbig_ref, w1_ref, lamt_ref, dexp_ref, y_ref):
    p = SSM_STATE
    width = SSM_GROUP * SSM_CHUNK
    uf = u_ref[0]
    u = uf.astype(BF16)
    n = u.shape[0]
    x = _dot(u, w1_ref[0])
    lam = lamt_ref[0]
    ar = lam[0:1, :]
    ai = lam[1:2, :]
    sgn = jnp.where(lax.broadcasted_iota(jnp.int32, (1, 2 * p), 1) < p, -1.0, 1.0)
    c_idx = lax.broadcasted_iota(jnp.int32, (n, 2 * p), 0) % chunks_per_seq
    d = 1
    while d < chunks_per_seq:
        xs = jnp.where(c_idx >= d, pltpu.roll(x, d, 0), 0.0)
        x = x + ar * xs + (sgn * ai) * pltpu.roll(xs, p, 1)
        ar, ai = ar * ar - ai * ai, 2.0 * ar * ai
        d *= 2
    x_prev = jnp.where(c_idx >= 1, pltpu.roll(x, 1, 0), 0.0)
    y = _dot(u, wbig_ref[0, 0:width, :]) + _dot(x_prev.astype(BF16), wbig_ref[0, width:, :])
    y_ref[0] = y + dexp_ref[0] * uf


def _ssm_main(u_g, wbig, w1, lamt, dexp, chunks_per_seq):
    g, n, width = u_g.shape
    p = SSM_STATE
    grp = lambda *s: pl.BlockSpec((1,) + s, lambda i: (i,) + (0,) * len(s))
    return pl.pallas_call(
        functools.partial(_ssm_main_kernel, chunks_per_seq),
        grid=(g,),
        in_specs=[grp(n, width), grp(width + 2 * p, width), grp(width, 2 * p), grp(SUBLANES, 2 * p),
                  grp(1, width)],
        out_specs=grp(n, width),
        out_shape=jax.ShapeDtypeStruct((g, n, width), F32),
        compiler_params=_params(),
        name="ssm_main",
    )(u_g, wbig, w1, lamt, dexp)


def _merge_kernel(x_ref, ing_ref, inb_ref, yg_ref, rowperm_ref, perm_ref, ma_ref, gb_ref, wglu_ref,
                  wmix_ref, g_ref, b_ref, o_ref):
    t = SSM_CHUNK
    per_tile = LANES // SSM_GROUP
    n_q = D_SSM // LANES
    rows = []
    for m in range(t // per_tile):
        for q in range(n_q):
            rows.append(jnp.concatenate(
                [yg_ref[q * per_tile + gam, :, m * LANES:(m + 1) * LANES] for gam in range(per_tile)],
                axis=1))
    swapped = _dot(jnp.concatenate(rows, axis=0).astype(BF16), perm_ref[...])
    blocks = []
    for m in range(t // per_tile):
        for i0 in range(per_tile):
            blocks.append(jnp.concatenate(
                [swapped[(m * n_q + q) * SUBLANES:(m * n_q + q + 1) * SUBLANES, i0 * LANES:(i0 + 1) * LANES]
                 for q in range(n_q)], axis=1))
    y_kc = jnp.concatenate(blocks, axis=0).astype(BF16)
    ys = _dot(rowperm_ref[...], y_kc).astype(BF16)
    z = _dot(ys, wglu_ref[...])
    yb = z[:, :D_MODEL] * jax.nn.sigmoid(z[:, D_MODEL:])
    merged = ma_ref[...].astype(F32) + gb_ref[...].astype(F32) * yb
    mix = _dot(merged.astype(BF16), wmix_ref[...])
    h = _layer_norm(x_ref[...], ing_ref[...], inb_ref[...])
    o_ref[...] = _layer_norm(DEEPNORM_ALPHA * h + mix, g_ref[...], b_ref[...])


def _merge(x2, ing, inb, y_g, ma, gb, w_glu, w_mix, g, b):
    n = x2.shape[0]
    chunks = ROW_TILE // SSM_CHUNK
    assert chunks == SUBLANES, "one row tile must fill the sublanes of a regrouped S5 block"
    row = lambda w: pl.BlockSpec((ROW_TILE, w), lambda i: (i, 0))
    vec = _const_spec((1, D_MODEL))
    return pl.pallas_call(
        _merge_kernel,
        grid=(n // ROW_TILE,),
        in_specs=[row(D_MODEL), vec, vec,
                  pl.BlockSpec((SSM_GROUPS, chunks, SSM_GROUP * SSM_CHUNK), lambda i: (0, i, 0)),
                  _const_spec((ROW_TILE, ROW_TILE)), _const_spec((SUBLANES * LANES, SUBLANES * LANES)),
                  row(D_MODEL), row(D_MODEL),
                  _const_spec((D_SSM, 2 * D_MODEL)), _const_spec((D_MODEL, D_MODEL)), vec, vec],
        out_specs=row(D_MODEL),
        out_shape=jax.ShapeDtypeStruct((n, D_MODEL), F32),
        compiler_params=_params(),
        name="merge_ln1",
    )(x2, ing, inb, y_g, _chunk_transpose_matrix(ROW_TILE, chunks), _block_swap_matrix(), ma, gb,
      w_glu, w_mix, g, b)


def _kv_kernel(m_ref, w_ref, k_ref, v_ref):
    kv = _dot(m_ref[...].astype(BF16), w_ref[...])
    k_ref[...] = kv[:, :D_MODEL].astype(BF16)
    v_ref[...] = kv[:, D_MODEL:].astype(BF16)


def _kv(mem2, w_kv, n_mem):
    n = mem2.shape[0]
    row = lambda w: pl.BlockSpec((n_mem, w), lambda i: (i, 0))
    return pl.pallas_call(
        _kv_kernel,
        grid=(n // n_mem,),
        in_specs=[row(D_MODEL), _const_spec((D_MODEL, 2 * D_MODEL))],
        out_specs=[row(D_MODEL), row(D_MODEL)],
        out_shape=[jax.ShapeDtypeStruct((n, D_MODEL), BF16)] * 2,
        compiler_params=_params(),
        name="mem_kv",
    )(mem2, w_kv)


def _attn_kernel(h_ref, k_ref, v_ref, wq_ref, wo_ref, g_ref, b_ref, o_ref):
    h = h_ref[...]
    q = _dot(h.astype(BF16), wq_ref[...]).astype(BF16)
    heads = []
    for hd in range(XA_HEADS):
        sl = slice(hd * XA_HEAD_DIM, (hd + 1) * XA_HEAD_DIM)
        s = lax.dot_general(q[:, sl], k_ref[:, sl], (((1,), (1,)), ((), ())),
                            preferred_element_type=F32) * (XA_HEAD_DIM ** -0.5)
        e = jnp.exp(s - jnp.max(s, axis=-1, keepdims=True))
        pr = e / jnp.sum(e, axis=-1, keepdims=True)
        heads.append(_dot(pr.astype(BF16), v_ref[:, sl]).astype(BF16))
    xa = _dot(jnp.concatenate(heads, axis=1), wo_ref[...])
    o_ref[...] = _layer_norm(DEEPNORM_ALPHA * h + xa, g_ref[...], b_ref[...])


def _attn(h1, k, v, wq, wo, g, b, seq, n_mem):
    n = h1.shape[0]
    tiles_per_seq = seq // ROW_TILE
    row = pl.BlockSpec((ROW_TILE, D_MODEL), lambda i: (i, 0))
    memb = pl.BlockSpec((n_mem, D_MODEL), lambda i: (i // tiles_per_seq, 0))
    vec = _const_spec((1, D_MODEL))
    sq = _const_spec((D_MODEL, D_MODEL))
    return pl.pallas_call(
        _attn_kernel,
        grid=(n // ROW_TILE,),
        in_specs=[row, memb, memb, sq, sq, vec, vec],
        out_specs=row,
        out_shape=jax.ShapeDtypeStruct((n, D_MODEL), F32),
        compiler_params=_params(),
        name="mem_attn_ln2",
    )(h1, k, v, wq, wo, g, b)


def _mlp_kernel(h_ref, wu_ref, wd_ref, g_ref, b_ref, o_ref):
    h = h_ref[...]
    hb = h.astype(BF16)
    ff = jnp.zeros((ROW_TILE, D_MODEL), F32)
    for c in range(D_FF // D_MODEL):
        sl = slice(c * D_MODEL, (c + 1) * D_MODEL)
        z = jnp.maximum(_dot(hb, wu_ref[:, sl]), 0.0)
        ff = ff + _dot((z * z).astype(BF16), wd_ref[sl, :])
    o_ref[...] = _layer_norm(DEEPNORM_ALPHA * h + ff, g_ref[...], b_ref[...])


def _mlp(h2, w_up, w_down, g, b):
    n = h2.shape[0]
    row = pl.BlockSpec((ROW_TILE, D_MODEL), lambda i: (i, 0))
    vec = _const_spec((1, D_MODEL))
    return pl.pallas_call(
        _mlp_kernel,
        grid=(n // ROW_TILE,),
        in_specs=[row, _const_spec((D_MODEL, D_FF)), _const_spec((D_FF, D_MODEL)), vec, vec],
        out_specs=row,
        out_shape=jax.ShapeDtypeStruct((n, D_MODEL), F32),
        compiler_params=_params(),
        name="mlp_ln3",
    )(h2, w_up, w_down, g, b)


def _layer(h_in, mem2, ing, inb, w_in, conv_dw, conv_db, conv_ng, conv_nb, w_conv_out, log_step,
           lam_re, lam_im, b_re, b_im, c_re, c_im, d, w_ssm_glu, w_mix_out, ln1_g, ln1_b,
           wq, wkv, wo, ln2_g, ln2_b, w_up, w_down, ln3_g, ln3_b, seq, n_mem):
    n = h_in.shape[0]
    t = SSM_CHUNK
    vec = lambda v: v.reshape(1, -1)
    ma, u_g, gb = _in_conv(h_in, ing, inb, w_in.astype(BF16), conv_dw, vec(conv_db), vec(conv_ng),
                           vec(conv_nb), w_conv_out.astype(BF16), seq)
    wbig, w1, lamt = _ssm_prep(log_step, lam_re, lam_im, b_re, b_im, c_re, c_im)
    dexp = jnp.tile(d.reshape(SSM_GROUPS, SSM_GROUP), (1, t))[:, None, :]
    y_g = _ssm_main(u_g, wbig, w1, lamt, dexp, seq // t)
    h1 = _merge(h_in, ing, inb, y_g, ma, gb, w_ssm_glu.astype(BF16), w_mix_out.astype(BF16),
                vec(ln1_g), vec(ln1_b))
    k, v = _kv(mem2, wkv.astype(BF16), n_mem)
    h2 = _attn(h1, k, v, wq.astype(BF16), wo.astype(BF16), vec(ln2_g), vec(ln2_b), seq, n_mem)
    return _mlp(h2, w_up.astype(BF16), w_down.astype(BF16), vec(ln3_g), vec(ln3_b))


def kernel(x, mem, in_norm_g, in_norm_b, w_in, conv_dw, conv_db, conv_norm_g, conv_norm_b, w_conv_out, ssm_log_step, ssm_lambda_re, ssm_lambda_im, ssm_b_re, ssm_b_im, ssm_c_re, ssm_c_im, ssm_d, w_ssm_glu, w_mix_out, ln1_g, ln1_b, xa_wq, xa_wkv, xa_wo, ln2_g, ln2_b, mlp_w_up, mlp_w_down, ln3_g, ln3_b):
    bsz, seq, dm = x.shape
    n_mem = mem.shape[1]
    depth = w_in.shape[0]
    assert depth == 1, "the input normalisation is fused into the single layer's kernels"
    assert dm == D_MODEL and seq % ROW_TILE == 0 and seq % SSM_CHUNK == 0
    x2 = x.reshape(bsz * seq, dm)
    mem2 = mem.reshape(bsz * n_mem, dm)
    out = _layer(x2, mem2, in_norm_g.reshape(1, -1), in_norm_b.reshape(1, -1), w_in[0], conv_dw[0],
                 conv_db[0], conv_norm_g[0], conv_norm_b[0], w_conv_out[0], ssm_log_step[0],
                 ssm_lambda_re[0], ssm_lambda_im[0], ssm_b_re[0], ssm_b_im[0], ssm_c_re[0],
                 ssm_c_im[0], ssm_d[0], w_ssm_glu[0], w_mix_out[0], ln1_g[0], ln1_b[0], xa_wq[0],
                 xa_wkv[0], xa_wo[0], ln2_g[0], ln2_b[0], mlp_w_up[0], mlp_w_down[0], ln3_g[0],
                 ln3_b[0], seq, n_mem)
    return out.reshape(bsz, seq, dm)
```

```python
import functools

import jax
import jax.numpy as jnp
from jax import lax
from jax.experimental import pallas as pl
from jax.experimental.pallas import tpu as pltpu

F32 = jnp.float32
BF16 = jnp.bfloat16

D_MODEL = 1024
D_CONV = 1024
CONV_K = 31
D_SSM = 512
SSM_GROUP = 16
SSM_GROUPS = 32
SSM_STATE = 64
XA_HEADS = 4
XA_HEAD_DIM = 256
D_FF = 4096
LN_EPS = 1e-5
DEEPNORM_ALPHA = 2.0 ** 0.25

LANES = 128
SUBLANES = 8
MXU_WIDTH = 256
SSM_CHUNK = 64
CONV_HALO = 32
CONV_ROWS = 128
ROW_TILE = 512
VMEM_LIMIT = 56 * 1024 * 1024


def _layer_norm(x, g, b):
    mu = jnp.mean(x, axis=-1, keepdims=True)
    xc = x - mu
    var = jnp.mean(xc * xc, axis=-1, keepdims=True)
    return xc * lax.rsqrt(var + LN_EPS) * g + b


def _dot(a, b):
    return jnp.dot(a, b, preferred_element_type=F32)


def _params(semantics="parallel"):
    return pltpu.CompilerParams(dimension_semantics=(semantics,), vmem_limit_bytes=VMEM_LIMIT)


def _const_spec(shape):
    nd = len(shape)
    return pl.BlockSpec(shape, lambda *_: (0,) * nd)


def _block_swap_matrix():
    idx = jnp.arange(SUBLANES * LANES)
    a, b, h = idx // LANES, (idx % LANES) // SSM_GROUP, idx % SSM_GROUP
    return (idx[None, :] == (b * LANES + a * SSM_GROUP + h)[:, None]).astype(BF16)


def _chunk_transpose_matrix(rows, inner):
    idx = jnp.arange(rows)
    src = (idx % (rows // inner)) * inner + idx // (rows // inner)
    return (idx[None, :] == src[:, None]).astype(BF16)


def _in_conv_kernel(tiles_per_seq, x_ref, ing_ref, inb_ref, w_ref, rowperm_ref, perm_ref, dw_ref,
                    db_ref, ng_ref, nb_ref, wco_ref, ma_ref, ug_ref, gb_ref, ext_ref, raw_ref):
    tm = ROW_TILE
    n_col = D_CONV // LANES
    o0, o1, o2, o3 = D_CONV, 2 * D_CONV, 2 * D_CONV + D_SSM, 2 * D_CONV + D_SSM + D_MODEL
    first = (pl.program_id(0) % tiles_per_seq) == 0

    @pl.when(first)
    def _():
        ext_ref[:, 0:CONV_HALO, :] = jnp.zeros((n_col, CONV_HALO, LANES), F32)

    @pl.when(jnp.logical_not(first))
    def _():
        ext_ref[:, 0:CONV_HALO, :] = ext_ref[:, tm:tm + CONV_HALO, :]

    h = _layer_norm(x_ref[...], ing_ref[...], inb_ref[...]).astype(BF16)

    def zero_after(v):
        bits = pltpu.bitcast(v, jnp.uint32)
        return lax.shift_right_logical(lax.shift_right_logical(bits, jnp.uint32(16)), jnp.uint32(16))

    def weights(cols_w, zero):
        w = w_ref[:, cols_w]
        if zero is None:
            return w
        head = pltpu.bitcast(pltpu.bitcast(w[0:2 * SUBLANES, 0:LANES], jnp.uint32) | zero, BF16)
        top = jnp.concatenate([head, w[0:2 * SUBLANES, LANES:]], axis=1)
        return jnp.concatenate([top, w[2 * SUBLANES:]], axis=0)

    def glu_piece(p):
        def run(zero):
            gate = _dot(h, weights(slice(o0 + p * MXU_WIDTH, o0 + (p + 1) * MXU_WIDTH), zero))
            u = _dot(h, weights(slice(p * MXU_WIDTH, (p + 1) * MXU_WIDTH), zero)) * jax.nn.sigmoid(gate)
            for half in range(MXU_WIDTH // LANES):
                ext_ref[p * (MXU_WIDTH // LANES) + half, CONV_HALO:, :] = (
                    u[:, half * LANES:(half + 1) * LANES])
            return gate[0:SUBLANES, 0:LANES]
        return run

    def gate_pieces(i0):
        def run(zero):
            for i in (i0, i0 + 1):
                raw = _dot(h, weights(slice(o2 + i * MXU_WIDTH, o2 + (i + 1) * MXU_WIDTH), zero))
                raw_ref[:, i * MXU_WIDTH:(i + 1) * MXU_WIDTH] = raw
            return raw[0:SUBLANES, 0:LANES]
        return run

    def ssm_input(zero):
        t = SSM_CHUNK
        per_tile = LANES // SSM_GROUP
        n_q = D_SSM // LANES
        n_c = tm // t
        s_kc = _dot(rowperm_ref[...], _dot(h, weights(slice(o1, o2), zero)).astype(BF16))
        rows = []
        for m in range(t // per_tile):
            for q in range(n_q):
                rows.append(jnp.concatenate(
                    [s_kc[(per_tile * m + k0) * n_c:(per_tile * m + k0 + 1) * n_c, q * LANES:(q + 1) * LANES]
                     for k0 in range(per_tile)], axis=1))
        swapped = _dot(jnp.concatenate(rows, axis=0).astype(BF16), perm_ref[...])
        for m in range(t // per_tile):
            for q in range(n_q):
                r0 = (m * n_q + q) * SUBLANES
                for gam in range(per_tile):
                    ug_ref[q * per_tile + gam, :, m * LANES:(m + 1) * LANES] = (
                        swapped[r0:r0 + SUBLANES, gam * LANES:(gam + 1) * LANES])
        return swapped[0:SUBLANES, 0:LANES]

    base = CONV_HALO - (CONV_K - 1)

    def conv_block(c, after):
        cs = slice(c * LANES, (c + 1) * LANES)
        taps = dw_ref[:, cs]
        if after is not None:
            zero = zero_after(after)[0:1, :]
            taps = [pltpu.bitcast(pltpu.bitcast(taps[j:j + 1, :], jnp.uint32) | zero, F32)
                    for j in range(CONV_K)]
        else:
            taps = [taps[j:j + 1, :] for j in range(CONV_K)]
        blocks = []
        for r0 in range(0, tm, CONV_ROWS):
            acc = jnp.broadcast_to(db_ref[:, cs], (CONV_ROWS, LANES))
            for j in range(CONV_K):
                acc = acc + taps[j] * ext_ref[c, base + j + r0:base + j + r0 + CONV_ROWS, :]
            blocks.append(acc)
        return jnp.concatenate(blocks, axis=0)

    side = [glu_piece(1), gate_pieces(0), glu_piece(2), gate_pieces(2), glu_piece(3), gate_pieces(4),
            ssm_input, gate_pieces(6)]
    assert len(side) == n_col
    glu_piece(0)(None)
    cols = []
    side_done = []
    for c in range(n_col):
        side_done.append(side[c](zero_after(cols[c - 2][0:SUBLANES, :]) if c >= 2 else None))
        cols.append(conv_block(c, side_done[c - 1] if c >= 1 else None))
    conv = _layer_norm(jnp.concatenate(cols, axis=1), ng_ref[...], nb_ref[...])
    conv = conv * jax.nn.sigmoid(conv)
    ya = _dot(conv.astype(BF16), wco_ref[...])
    ma_ref[...] = (jax.nn.sigmoid(raw_ref[:, :D_MODEL]) * ya).astype(BF16)
    gb_ref[...] = jax.nn.sigmoid(raw_ref[:, D_MODEL:]).astype(BF16)


def _in_conv(x2, ing, inb, w_in, dw, db, ng, nb, w_conv_out, seq):
    n = x2.shape[0]
    tm = ROW_TILE
    d_in = w_in.shape[1]
    chunks = tm // SSM_CHUNK
    assert chunks == SUBLANES, "one row tile must fill the sublanes of a regrouped S5 block"
    row = lambda w: pl.BlockSpec((tm, w), lambda i: (i, 0))
    vec = _const_spec((1, D_MODEL))
    return pl.pallas_call(
        functools.partial(_in_conv_kernel, seq // tm),
        grid=(n // tm,),
        in_specs=[row(D_MODEL), vec, vec, _const_spec((D_MODEL, d_in)),
                  _const_spec((tm, tm)), _const_spec((SUBLANES * LANES, SUBLANES * LANES)),
                  _const_spec((CONV_K, D_CONV)), vec, vec, vec, _const_spec((D_CONV, D_MODEL))],
        out_specs=[row(D_MODEL),
                   pl.BlockSpec((SSM_GROUPS, chunks, SSM_GROUP * SSM_CHUNK), lambda i: (0, i, 0)),
                   row(D_MODEL)],
        out_shape=[jax.ShapeDtypeStruct((n, D_MODEL), BF16),
                   jax.ShapeDtypeStruct((SSM_GROUPS, n // SSM_CHUNK, SSM_GROUP * SSM_CHUNK), F32),
                   jax.ShapeDtypeStruct((n, D_MODEL), BF16)],
        scratch_shapes=[pltpu.VMEM((D_CONV // LANES, CONV_HALO + tm, LANES), F32),
                        pltpu.VMEM((tm, 2 * D_MODEL), F32)],
        compiler_params=_params("arbitrary"),
        name="in_conv",
    )(x2, ing, inb, w_in, _chunk_transpose_matrix(tm, SSM_CHUNK), _block_swap_matrix(), dw, db, ng, nb,
      w_conv_out)


def _ssm_prep_kernel(lsc_ref, lrc_ref, lic_ref, lsr_ref, lrr_ref, lir_ref, btr_ref, bti_ref,
                     cer_ref, cei_ref, wbig_ref, w1_ref, lamt_ref):
    t = SSM_CHUNK
    p = SSM_STATE
    hh = SSM_GROUP
    width = hh * t
    a_c = lrc_ref[0] * jnp.exp(lsc_ref[0])
    th_c = lic_ref[0] * jnp.exp(lsc_ref[0])
    hp = lax.Precision.HIGHEST
    j = (lax.broadcasted_iota(jnp.int32, (p, LANES), 1) % t).astype(F32)
    mag = jnp.exp(a_c * j)
    l0r = mag * jnp.cos(th_c * j)
    l0i = mag * jnp.sin(th_c * j)
    onehot = (lax.broadcasted_iota(jnp.int32, (t, width), 0)
              == lax.broadcasted_iota(jnp.int32, (t, width), 1) // hh).astype(F32)
    l0r = jnp.dot(l0r[:, :t], onehot, precision=hp, preferred_element_type=F32)
    l0i = jnp.dot(l0i[:, :t], onehot, precision=hp, preferred_element_type=F32)
    cer = cer_ref[0]
    cei = cei_ref[0]
    cl0r = cer * l0r - cei * l0i
    cl0i = cer * l0i + cei * l0r
    lam_r = jnp.exp(a_c) * jnp.cos(th_c)
    lam_i = jnp.exp(a_c) * jnp.sin(th_c)
    cl1r = cl0r * lam_r - cl0i * lam_i
    cl1i = cl0r * lam_i + cl0i * lam_r
    wbig_ref[0, width:width + p, :] = cl1r.astype(BF16)
    wbig_ref[0, width + p:width + 2 * p, :] = (-cl1i).astype(BF16)

    step_r = jnp.exp(lsr_ref[0])
    lr = lrr_ref[0]
    li = lir_ref[0]
    a_r = lr * step_r
    th_r = li * step_r
    ar = jnp.exp(a_r) * jnp.cos(th_r)
    ai = jnp.exp(a_r) * jnp.sin(th_r)
    den = lr * lr + li * li
    nr = ar - 1.0
    cr = (nr * lr + ai * li) / den
    ci = (ai * lr - nr * li) / den
    btr = btr_ref[0]
    bti = bti_ref[0]
    bbr = cr * btr - ci * bti
    bbi = cr * bti + ci * btr

    g2 = (jnp.dot(bbr[:, :p], cl0r, precision=hp, preferred_element_type=F32)
          - jnp.dot(bbi[:, :p], cl0i, precision=hp, preferred_element_type=F32))
    col = lax.broadcasted_iota(jnp.int32, (hh, width), 1)
    for k in range(t):
        taps = g2 if k == 0 else pltpu.roll(g2, k * hh, 1)
        wbig_ref[0, k * hh:(k + 1) * hh, :] = jnp.where(col >= k * hh, taps, 0.0).astype(BF16)

    kk = ((t - 1) - lax.broadcasted_iota(jnp.int32, (t, 2 * p), 0)).astype(F32)
    magk = jnp.exp(a_r * kk)
    lkr = magk * jnp.cos(th_r * kk)
    lki = magk * jnp.sin(th_r * kk)
    real_half = lax.broadcasted_iota(jnp.int32, (hh, 2 * p), 1) < p
    for k in range(t):
        er = lkr[k:k + 1, :]
        ei = lki[k:k + 1, :]
        w1_ref[0, k * hh:(k + 1) * hh, :] = jnp.where(real_half, er * bbr - ei * bbi,
                                                      er * bbi + ei * bbr).astype(BF16)

    magt = jnp.exp(a_r * float(t))
    ltr = magt * jnp.cos(th_r * float(t))
    lti = magt * jnp.sin(th_r * float(t))
    sub = lax.broadcasted_iota(jnp.int32, (SUBLANES, 2 * p), 0)
    lamt_ref[0] = jnp.where(sub == 0, ltr, jnp.where(sub == 1, lti, 0.0))


def _ssm_prep(log_step, lam_re, lam_im, b_re, b_im, c_re, c_im):
    g, p, hh, t = SSM_GROUPS, SSM_STATE, SSM_GROUP, SSM_CHUNK
    width = hh * t
    col = lambda v: v[:, :, None]
    row2 = lambda v: jnp.concatenate([v, v], axis=-1)[:, None, :]
    ls = jnp.broadcast_to(log_step[:, None], (g, p))
    bt = lambda b: jnp.concatenate([b.transpose(0, 2, 1)] * 2, axis=-1)
    cexp = lambda c: jnp.tile(c.transpose(0, 2, 1), (1, 1, t))
    grp = lambda *s: pl.BlockSpec((1,) + s, lambda i: (i,) + (0,) * len(s))
    return pl.pallas_call(
        _ssm_prep_kernel,
        grid=(g,),
        in_specs=[grp(p, 1)] * 3 + [grp(1, 2 * p)] * 3 + [grp(hh, 2 * p)] * 2 + [grp(p, width)] * 2,
        out_specs=[grp(width + 2 * p, width), grp(width, 2 * p), grp(SUBLANES, 2 * p)],
        out_shape=[jax.ShapeDtypeStruct((g, width + 2 * p, width), BF16),
                   jax.ShapeDtypeStruct((g, width, 2 * p), BF16),
                   jax.ShapeDtypeStruct((g, SUBLANES, 2 * p), F32)],
        compiler_params=_params(),
        name="ssm_prep",
    )(col(ls), col(lam_re), col(lam_im), row2(ls), row2(lam_re), row2(lam_im),
      bt(b_re), bt(b_im), cexp(c_re), cexp(c_im))


def _ssm_main_kernel(chunks_per_seq, u_ref, wbig_ref, w1_ref, lamt_ref, dexp_ref, y_ref):
    p = SSM_STATE
    width = SSM_GROUP * SSM_CHUNK
    uf = u_ref[0]
    u = uf.astype(BF16)
    n = u.shape[0]
    x = _dot(u, w1_ref[0])
    lam = lamt_ref[0]
    ar = lam[0:1, :]
    ai = lam[1:2, :]
    sgn = jnp.where(lax.broadcasted_iota(jnp.int32, (1, 2 * p), 1) < p, -1.0, 1.0)
    c_idx = lax.broadcasted_iota(jnp.int32, (n, 2 * p), 0) % chunks_per_seq
    d = 1
    while d < chunks_per_seq:
        xs = jnp.where(c_idx >= d, pltpu.roll(x, d, 0), 0.0)
        x = x + ar * xs + (sgn * ai) * pltpu.roll(xs, p, 1)
        ar, ai = ar * ar - ai * ai, 2.0 * ar * ai
        d *= 2
    x_prev = jnp.where(c_idx >= 1, pltpu.roll(x, 1, 0), 0.0)
    y = _dot(u, wbig_ref[0, 0:width, :]) + _dot(x_prev.astype(BF16), wbig_ref[0, width:, :])
    y_ref[0] = y + dexp_ref[0] * uf


def _ssm_main(u_g, wbig, w1, lamt, dexp, chunks_per_seq):
    g, n, width = u_g.shape
    p = SSM_STATE
    grp = lambda *s: pl.BlockSpec((1,) + s, lambda i: (i,) + (0,) * len(s))
    return pl.pallas_call(
        functools.partial(_ssm_main_kernel, chunks_per_seq),
        grid=(g,),
        in_specs=[grp(n, width), grp(width + 2 * p, width), grp(width, 2 * p), grp(SUBLANES, 2 * p),
                  grp(1, width)],
        out_specs=grp(n, width),
        out_shape=jax.ShapeDtypeStruct((g, n, width), F32),
        compiler_params=_params(),
        name="ssm_main",
    )(u_g, wbig, w1, lamt, dexp)


def _merge_kernel(x_ref, ing_ref, inb_ref, yg_ref, rowperm_ref, perm_ref, ma_ref, gb_ref, wglu_ref,
                  wmix_ref, g_ref, b_ref, o_ref):
    t = SSM_CHUNK
    per_tile = LANES // SSM_GROUP
    n_q = D_SSM // LANES
    rows = []
    for m in range(t // per_tile):
        for q in range(n_q):
            rows.append(jnp.concatenate(
                [yg_ref[q * per_tile + gam, :, m * LANES:(m + 1) * LANES] for gam in range(per_tile)],
                axis=1))
    swapped = _dot(jnp.concatenate(rows, axis=0).astype(BF16), perm_ref[...])
    blocks = []
    for m in range(t // per_tile):
        for i0 in range(per_tile):
            blocks.append(jnp.concatenate(
                [swapped[(m * n_q + q) * SUBLANES:(m * n_q + q + 1) * SUBLANES, i0 * LANES:(i0 + 1) * LANES]
                 for q in range(n_q)], axis=1))
    y_kc = jnp.concatenate(blocks, axis=0).astype(BF16)
    ys = _dot(rowperm_ref[...], y_kc).astype(BF16)
    z = _dot(ys, wglu_ref[...])
    yb = z[:, :D_MODEL] * jax.nn.sigmoid(z[:, D_MODEL:])
    merged = ma_ref[...].astype(F32) + gb_ref[...].astype(F32) * yb
    mix = _dot(merged.astype(BF16), wmix_ref[...])
    h = _layer_norm(x_ref[...], ing_ref[...], inb_ref[...])
    o_ref[...] = _layer_norm(DEEPNORM_ALPHA * h + mix, g_ref[...], b_ref[...])


def _merge(x2, ing, inb, y_g, ma, gb, w_glu, w_mix, g, b):
    n = x2.shape[0]
    chunks = ROW_TILE // SSM_CHUNK
    assert chunks == SUBLANES, "one row tile must fill the sublanes of a regrouped S5 block"
    row = lambda w: pl.BlockSpec((ROW_TILE, w), lambda i: (i, 0))
    vec = _const_spec((1, D_MODEL))
    return pl.pallas_call(
        _merge_kernel,
        grid=(n // ROW_TILE,),
        in_specs=[row(D_MODEL), vec, vec,
                  pl.BlockSpec((SSM_GROUPS, chunks, SSM_GROUP * SSM_CHUNK), lambda i: (0, i, 0)),
                  _const_spec((ROW_TILE, ROW_TILE)), _const_spec((SUBLANES * LANES, SUBLANES * LANES)),
                  row(D_MODEL), row(D_MODEL),
                  _const_spec((D_SSM, 2 * D_MODEL)), _const_spec((D_MODEL, D_MODEL)), vec, vec],
        out_specs=row(D_MODEL),
        out_shape=jax.ShapeDtypeStruct((n, D_MODEL), F32),
        compiler_params=_params(),
        name="merge_ln1",
    )(x2, ing, inb, y_g, _chunk_transpose_matrix(ROW_TILE, chunks), _block_swap_matrix(), ma, gb,
      w_glu, w_mix, g, b)


def _kv_kernel(m_ref, w_ref, k_ref, v_ref):
    kv = _dot(m_ref[...].astype(BF16), w_ref[...])
    k_ref[...] = kv[:, :D_MODEL].astype(BF16)
    v_ref[...] = kv[:, D_MODEL:].astype(BF16)


def _kv(mem2, w_kv, n_mem):
    n = mem2.shape[0]
    row = lambda w: pl.BlockSpec((n_mem, w), lambda i: (i, 0))
    return pl.pallas_call(
        _kv_kernel,
        grid=(n // n_mem,),
        in_specs=[row(D_MODEL), _const_spec((D_MODEL, 2 * D_MODEL))],
        out_specs=[row(D_MODEL), row(D_MODEL)],
        out_shape=[jax.ShapeDtypeStruct((n, D_MODEL), BF16)] * 2,
        compiler_params=_params(),
        name="mem_kv",
    )(mem2, w_kv)


def _attn_kernel(h_ref, k_ref, v_ref, wq_ref, wo_ref, g_ref, b_ref, o_ref):
    h = h_ref[...]
    q = _dot(h.astype(BF16), wq_ref[...]).astype(BF16)
    heads = []
    for hd in range(XA_HEADS):
        sl = slice(hd * XA_HEAD_DIM, (hd + 1) * XA_HEAD_DIM)
        s = lax.dot_general(q[:, sl], k_ref[:, sl], (((1,), (1,)), ((), ())),
                            preferred_element_type=F32) * (XA_HEAD_DIM ** -0.5)
        e = jnp.exp(s - jnp.max(s, axis=-1, keepdims=True))
        pr = e / jnp.sum(e, axis=-1, keepdims=True)
        heads.append(_dot(pr.astype(BF16), v_ref[:, sl]).astype(BF16))
    xa = _dot(jnp.concatenate(heads, axis=1), wo_ref[...])
    o_ref[...] = _layer_norm(DEEPNORM_ALPHA * h + xa, g_ref[...], b_ref[...])


def _attn(h1, k, v, wq, wo, g, b, seq, n_mem):
    n = h1.shape[0]
    tiles_per_seq = seq // ROW_TILE
    row = pl.BlockSpec((ROW_TILE, D_MODEL), lambda i: (i, 0))
    memb = pl.BlockSpec((n_mem, D_MODEL), lambda i: (i // tiles_per_seq, 0))
    vec = _const_spec((1, D_MODEL))
    sq = _const_spec((D_MODEL, D_MODEL))
    return pl.pallas_call(
        _attn_kernel,
        grid=(n // ROW_TILE,),
        in_specs=[row, memb, memb, sq, sq, vec, vec],
        out_specs=row,
        out_shape=jax.ShapeDtypeStruct((n, D_MODEL), F32),
        compiler_params=_params(),
        name="mem_attn_ln2",
    )(h1, k, v, wq, wo, g, b)


def _mlp_kernel(h_ref, wu_ref, wd_ref, g_ref, b_ref, o_ref):
    h = h_ref[...]
    hb = h.astype(BF16)
    ff = jnp.zeros((ROW_TILE, D_MODEL), F32)
    for c in range(D_FF // D_MODEL):
        sl = slice(c * D_MODEL, (c + 1) * D_MODEL)
        z = jnp.maximum(_dot(hb, wu_ref[:, sl]), 0.0)
        ff = ff + _dot((z * z).astype(BF16), wd_ref[sl, :])
    o_ref[...] = _layer_norm(DEEPNORM_ALPHA * h + ff, g_ref[...], b_ref[...])


def _mlp(h2, w_up, w_down, g, b):
    n = h2.shape[0]
    row = pl.BlockSpec((ROW_TILE, D_MODEL), lambda i: (i, 0))
    vec = _const_spec((1, D_MODEL))
    return pl.pallas_call(
        _mlp_kernel,
        grid=(n // ROW_TILE,),
        in_specs=[row, _const_spec((D_MODEL, D_FF)), _const_spec((D_FF, D_MODEL)), vec, vec],
        out_specs=row,
        out_shape=jax.ShapeDtypeStruct((n, D_MODEL), F32),
        compiler_params=_params(),
        name="mlp_ln3",
    )(h2, w_up, w_down, g, b)


def _layer(h_in, mem2, ing, inb, w_in, conv_dw, conv_db, conv_ng, conv_nb, w_conv_out, log_step,
           lam_re, lam_im, b_re, b_im, c_re, c_im, d, w_ssm_glu, w_mix_out, ln1_g, ln1_b,
           wq, wkv, wo, ln2_g, ln2_b, w_up, w_down, ln3_g, ln3_b, seq, n_mem):
    n = h_in.shape[0]
    t = SSM_CHUNK
    vec = lambda v: v.reshape(1, -1)
    ma, u_g, gb = _in_conv(h_in, ing, inb, w_in.astype(BF16), conv_dw, vec(conv_db), vec(conv_ng),
                           vec(conv_nb), w_conv_out.astype(BF16), seq)
    wbig, w1, lamt = _ssm_prep(log_step, lam_re, lam_im, b_re, b_im, c_re, c_im)
    dexp = jnp.tile(d.reshape(SSM_GROUPS, SSM_GROUP), (1, t))[:, None, :]
    y_g = _ssm_main(u_g, wbig, w1, lamt, dexp, seq // t)
    h1 = _merge(h_in, ing, inb, y_g, ma, gb, w_ssm_glu.astype(BF16), w_mix_out.astype(BF16),
                vec(ln1_g), vec(ln1_b))
    k, v = _kv(mem2, wkv.astype(BF16), n_mem)
    h2 = _attn(h1, k, v, wq.astype(BF16), wo.astype(BF16), vec(ln2_g), vec(ln2_b), seq, n_mem)
    return _mlp(h2, w_up.astype(BF16), w_down.astype(BF16), vec(ln3_g), vec(ln3_b))


def kernel(x, mem, in_norm_g, in_norm_b, w_in, conv_dw, conv_db, conv_norm_g, conv_norm_b, w_conv_out, ssm_log_step, ssm_lambda_re, ssm_lambda_im, ssm_b_re, ssm_b_im, ssm_c_re, ssm_c_im, ssm_d, w_ssm_glu, w_mix_out, ln1_g, ln1_b, xa_wq, xa_wkv, xa_wo, ln2_g, ln2_b, mlp_w_up, mlp_w_down, ln3_g, ln3_b):
    bsz, seq, dm = x.shape
    n_mem = mem.shape[1]
    depth = w_in.shape[0]
    assert depth == 1, "the input normalisation is fused into the single layer's kernels"
    assert dm == D_MODEL and seq % ROW_TILE == 0 and seq % SSM_CHUNK == 0
    x2 = x.reshape(bsz * seq, dm)
    mem2 = mem.reshape(bsz * n_mem, dm)
    out = _layer(x2, mem2, in_norm_g.reshape(1, -1), in_norm_b.reshape(1, -1), w_in[0], conv_dw[0],
                 conv_db[0], conv_norm_g[0], conv_norm_b[0], w_conv_out[0], ssm_log_step[0],
                 ssm_lambda_re[0], ssm_lambda_im[0], ssm_b_re[0], ssm_b_im[0], ssm_c_re[0],
                 ssm_c_im[0], ssm_d[0], w_ssm_glu[0], w_mix_out[0], ln1_g[0], ln1_b[0], xa_wq[0],
                 xa_wkv[0], xa_wo[0], ln2_g[0], ln2_b[0], mlp_w_up[0], mlp_w_down[0], ln3_g[0],
                 ln3_b[0], seq, n_mem)
    return out.reshape(bsz, seq, dm)
```

```python
import functools

import jax
import jax.numpy as jnp
from jax import lax
from jax.experimental import pallas as pl
from jax.experimental.pallas import tpu as pltpu

F32 = jnp.float32
BF16 = jnp.bfloat16

D_MODEL = 1024
D_CONV = 1024
CONV_K = 31
D_SSM = 512
SSM_GROUP = 16
SSM_GROUPS = 32
SSM_STATE = 64
XA_HEADS = 4
XA_HEAD_DIM = 256
D_FF = 4096
LN_EPS = 1e-5
DEEPNORM_ALPHA = 2.0 ** 0.25

LANES = 128
SUBLANES = 8
MXU_WIDTH = 256
SSM_CHUNK = 64
CONV_HALO = 32
CONV_ROWS = 128
ROW_TILE = 512
VMEM_LIMIT = 56 * 1024 * 1024


def _layer_norm(x, g, b):
    mu = jnp.mean(x, axis=-1, keepdims=True)
    xc = x - mu
    var = jnp.mean(xc * xc, axis=-1, keepdims=True)
    return xc * lax.rsqrt(var + LN_EPS) * g + b


def _dot(a, b):
    return jnp.dot(a, b, preferred_element_type=F32)


def _params(semantics="parallel"):
    return pltpu.CompilerParams(dimension_semantics=(semantics,), vmem_limit_bytes=VMEM_LIMIT)


def _const_spec(shape):
    nd = len(shape)
    return pl.BlockSpec(shape, lambda *_: (0,) * nd)


def _block_swap_matrix():
    idx = jnp.arange(SUBLANES * LANES)
    a, b, h = idx // LANES, (idx % LANES) // SSM_GROUP, idx % SSM_GROUP
    return (idx[None, :] == (b * LANES + a * SSM_GROUP + h)[:, None]).astype(BF16)


def _chunk_transpose_matrix(rows, inner):
    idx = jnp.arange(rows)
    src = (idx % (rows // inner)) * inner + idx // (rows // inner)
    return (idx[None, :] == src[:, None]).astype(BF16)


def _in_conv_kernel(tiles_per_seq, x_ref, ing_ref, inb_ref, w_ref, rowperm_ref, perm_ref, dw_ref,
                    db_ref, ng_ref, nb_ref, wco_ref, ma_ref, ug_ref, gb_ref, ext_ref, raw_ref):
    tm = ROW_TILE
    n_col = D_CONV // LANES
    o0, o1, o2, o3 = D_CONV, 2 * D_CONV, 2 * D_CONV + D_SSM, 2 * D_CONV + D_SSM + D_MODEL
    first = (pl.program_id(0) % tiles_per_seq) == 0

    @pl.when(first)
    def _():
        ext_ref[:, 0:CONV_HALO, :] = jnp.zeros((n_col, CONV_HALO, LANES), F32)

    @pl.when(jnp.logical_not(first))
    def _():
        ext_ref[:, 0:CONV_HALO, :] = ext_ref[:, tm:tm + CONV_HALO, :]

    h = _layer_norm(x_ref[...], ing_ref[...], inb_ref[...]).astype(BF16)

    def zero_after(v):
        bits = pltpu.bitcast(v, jnp.uint32)
        return lax.shift_right_logical(lax.shift_right_logical(bits, jnp.uint32(16)), jnp.uint32(16))

    def weights(cols_w, zero):
        w = w_ref[:, cols_w]
        if zero is None:
            return w
        head = pltpu.bitcast(pltpu.bitcast(w[0:2 * SUBLANES, 0:LANES], jnp.uint32) | zero, BF16)
        top = jnp.concatenate([head, w[0:2 * SUBLANES, LANES:]], axis=1)
        return jnp.concatenate([top, w[2 * SUBLANES:]], axis=0)

    def glu_piece(p):
        def run(zero):
            gate = _dot(h, weights(slice(o0 + p * MXU_WIDTH, o0 + (p + 1) * MXU_WIDTH), zero))
            u = _dot(h, weights(slice(p * MXU_WIDTH, (p + 1) * MXU_WIDTH), zero)) * jax.nn.sigmoid(gate)
            for half in range(MXU_WIDTH // LANES):
                ext_ref[p * (MXU_WIDTH // LANES) + half, CONV_HALO:, :] = (
                    u[:, half * LANES:(half + 1) * LANES])
            return gate[0:SUBLANES, 0:LANES]
        return run

    def gate_pieces(i0):
        def run(zero):
            for i in (i0, i0 + 1):
                raw = _dot(h, weights(slice(o2 + i * MXU_WIDTH, o2 + (i + 1) * MXU_WIDTH), zero))
                raw_ref[:, i * MXU_WIDTH:(i + 1) * MXU_WIDTH] = raw
            return raw[0:SUBLANES, 0:LANES]
        return run

    def ssm_input(zero):
        t = SSM_CHUNK
        per_tile = LANES // SSM_GROUP
        n_q = D_SSM // LANES
        n_c = tm // t
        s_kc = _dot(rowperm_ref[...], _dot(h, weights(slice(o1, o2), zero)).astype(BF16))
        rows = []
        for m in range(t // per_tile):
            for q in range(n_q):
                rows.append(jnp.concatenate(
                    [s_kc[(per_tile * m + k0) * n_c:(per_tile * m + k0 + 1) * n_c, q * LANES:(q + 1) * LANES]
                     for k0 in range(per_tile)], axis=1))
        swapped = _dot(jnp.concatenate(rows, axis=0).astype(BF16), perm_ref[...])
        for m in range(t // per_tile):
            for q in range(n_q):
                r0 = (m * n_q + q) * SUBLANES
                for gam in range(per_tile):
                    ug_ref[q * per_tile + gam, :, m * LANES:(m + 1) * LANES] = (
                        swapped[r0:r0 + SUBLANES, gam * LANES:(gam + 1) * LANES])
        return swapped[0:SUBLANES, 0:LANES]

    base = CONV_HALO - (CONV_K - 1)

    def conv_block(c, after):
        cs = slice(c * LANES, (c + 1) * LANES)
        taps = dw_ref[:, cs]
        if after is not None:
            zero = zero_after(after)[0:1, :]
            taps = [pltpu.bitcast(pltpu.bitcast(taps[j:j + 1, :], jnp.uint32) | zero, F32)
                    for j in range(CONV_K)]
        else:
            taps = [taps[j:j + 1, :] for j in range(CONV_K)]
        blocks = []
        for r0 in range(0, tm, CONV_ROWS):
            acc = jnp.broadcast_to(db_ref[:, cs], (CONV_ROWS, LANES))
            for j in range(CONV_K):
                acc = acc + taps[j] * ext_ref[c, base + j + r0:base + j + r0 + CONV_ROWS, :]
            blocks.append(acc)
        return jnp.concatenate(blocks, axis=0)

    side = [glu_piece(1), gate_pieces(0), glu_piece(2), gate_pieces(2), glu_piece(3), gate_pieces(4),
            ssm_input, gate_pieces(6)]
    assert len(side) == n_col
    glu_piece(0)(None)
    cols = []
    side_done = []
    for c in range(n_col):
        side_done.append(side[c](zero_after(cols[c - 2][0:SUBLANES, :]) if c >= 2 else None))
        cols.append(conv_block(c, side_done[c - 1] if c >= 1 else None))
    conv = _layer_norm(jnp.concatenate(cols, axis=1), ng_ref[...], nb_ref[...])
    conv = conv * jax.nn.sigmoid(conv)
    ya = _dot(conv.astype(BF16), wco_ref[...])
    ma_ref[...] = (jax.nn.sigmoid(raw_ref[:, :D_MODEL]) * ya).astype(BF16)
    gb_ref[...] = jax.nn.sigmoid(raw_ref[:, D_MODEL:]).astype(BF16)


def _in_conv(x2, ing, inb, w_in, dw, db, ng, nb, w_conv_out, seq):
    n = x2.shape[0]
    tm = ROW_TILE
    d_in = w_in.shape[1]
    chunks = tm // SSM_CHUNK
    assert chunks == SUBLANES, "one row tile must fill the sublanes of a regrouped S5 block"
    row = lambda w: pl.BlockSpec((tm, w), lambda i: (i, 0))
    vec = _const_spec((1, D_MODEL))
    return pl.pallas_call(
        functools.partial(_in_conv_kernel, seq // tm),
        grid=(n // tm,),
        in_specs=[row(D_MODEL), vec, vec, _const_spec((D_MODEL, d_in)),
                  _const_spec((tm, tm)), _const_spec((SUBLANES * LANES, SUBLANES * LANES)),
                  _const_spec((CONV_K, D_CONV)), vec, vec, vec, _const_spec((D_CONV, D_MODEL))],
        out_specs=[row(D_MODEL),
                   pl.BlockSpec((SSM_GROUPS, chunks, SSM_GROUP * SSM_CHUNK), lambda i: (0, i, 0)),
                   row(D_MODEL)],
        out_shape=[jax.ShapeDtypeStruct((n, D_MODEL), BF16),
                   jax.ShapeDtypeStruct((SSM_GROUPS, n // SSM_CHUNK, SSM_GROUP * SSM_CHUNK), F32),
                   jax.ShapeDtypeStruct((n, D_MODEL), BF16)],
        scratch_shapes=[pltpu.VMEM((D_CONV // LANES, CONV_HALO + tm, LANES), F32),
                        pltpu.VMEM((tm, 2 * D_MODEL), F32)],
        compiler_params=_params("arbitrary"),
        name="in_conv",
    )(x2, ing, inb, w_in, _chunk_transpose_matrix(tm, SSM_CHUNK), _block_swap_matrix(), dw, db, ng, nb,
      w_conv_out)


def _ssm_operators(lsc_ref, lrc_ref, lic_ref, lsr_ref, lrr_ref, lir_ref, btr_ref, bti_ref,
                   ctr_ref, cti_ref, wbig_ref, w1_ref, lamt_ref):
    t = SSM_CHUNK
    p = SSM_STATE
    hh = SSM_GROUP
    width = hh * t
    a_c = lrc_ref[0] * jnp.exp(lsc_ref[0])
    th_c = lic_ref[0] * jnp.exp(lsc_ref[0])
    hp = lax.Precision.HIGHEST
    j = (lax.broadcasted_iota(jnp.int32, (p, LANES), 1) % t).astype(F32)
    mag = jnp.exp(a_c * j)
    l0r = mag * jnp.cos(th_c * j)
    l0i = mag * jnp.sin(th_c * j)

    def widen(x, onehot):
        hi = x.astype(BF16)
        rest = x - hi.astype(F32)
        mid = rest.astype(BF16)
        lo = (rest - mid.astype(F32)).astype(BF16)
        return _dot(hi, onehot) + _dot(mid, onehot) + _dot(lo, onehot)

    col_t = lax.broadcasted_iota(jnp.int32, (t, width), 1)
    by_step = (lax.broadcasted_iota(jnp.int32, (t, width), 0) == col_t // hh).astype(BF16)
    col_h = lax.broadcasted_iota(jnp.int32, (hh, width), 1)
    by_chan = (lax.broadcasted_iota(jnp.int32, (hh, width), 0) == col_h % hh).astype(BF16)
    l0r = widen(l0r[:, :t], by_step)
    l0i = widen(l0i[:, :t], by_step)
    cer = widen(ctr_ref[0], by_chan)
    cei = widen(cti_ref[0], by_chan)
    cl0r = cer * l0r - cei * l0i
    cl0i = cer * l0i + cei * l0r
    lam_r = jnp.exp(a_c) * jnp.cos(th_c)
    lam_i = jnp.exp(a_c) * jnp.sin(th_c)
    cl1r = cl0r * lam_r - cl0i * lam_i
    cl1i = cl0r * lam_i + cl0i * lam_r
    wbig_ref[0, width:width + p, :] = cl1r.astype(BF16)
    wbig_ref[0, width + p:width + 2 * p, :] = (-cl1i).astype(BF16)

    step_r = jnp.exp(lsr_ref[0])
    lr = lrr_ref[0]
    li = lir_ref[0]
    a_r = lr * step_r
    th_r = li * step_r
    ar = jnp.exp(a_r) * jnp.cos(th_r)
    ai = jnp.exp(a_r) * jnp.sin(th_r)
    den = lr * lr + li * li
    nr = ar - 1.0
    cr = (nr * lr + ai * li) / den
    ci = (ai * lr - nr * li) / den
    btr = btr_ref[0]
    bti = bti_ref[0]
    bbr = cr * btr - ci * bti
    bbi = cr * bti + ci * btr

    g2 = (jnp.dot(bbr[:, :p], cl0r, precision=hp, preferred_element_type=F32)
          - jnp.dot(bbi[:, :p], cl0i, precision=hp, preferred_element_type=F32))
    col = lax.broadcasted_iota(jnp.int32, (hh, width), 1)
    for k in range(t):
        taps = g2 if k == 0 else pltpu.roll(g2, k * hh, 1)
        wbig_ref[0, k * hh:(k + 1) * hh, :] = jnp.where(col >= k * hh, taps, 0.0).astype(BF16)

    kk = ((t - 1) - lax.broadcasted_iota(jnp.int32, (t, 2 * p), 0)).astype(F32)
    magk = jnp.exp(a_r * kk)
    lkr = magk * jnp.cos(th_r * kk)
    lki = magk * jnp.sin(th_r * kk)
    real_half = lax.broadcasted_iota(jnp.int32, (hh, 2 * p), 1) < p
    for k in range(t):
        er = lkr[k:k + 1, :]
        ei = lki[k:k + 1, :]
        w1_ref[0, k * hh:(k + 1) * hh, :] = jnp.where(real_half, er * bbr - ei * bbi,
                                                      er * bbi + ei * bbr).astype(BF16)

    magt = jnp.exp(a_r * float(t))
    ltr = magt * jnp.cos(th_r * float(t))
    lti = magt * jnp.sin(th_r * float(t))
    sub = lax.broadcasted_iota(jnp.int32, (SUBLANES, 2 * p), 0)
    lamt_ref[0] = jnp.where(sub == 0, ltr, jnp.where(sub == 1, lti, 0.0))


def _ssm_apply(chunks_per_seq, u_ref, wbig_ref, w1_ref, lamt_ref, dexp_ref, y_ref):
    p = SSM_STATE
    width = SSM_GROUP * SSM_CHUNK
    uf = u_ref[0]
    u = uf.astype(BF16)
    n = u.shape[0]
    x = _dot(u, w1_ref[0])
    lam = lamt_ref[0]
    ar = lam[0:1, :]
    ai = lam[1:2, :]
    sgn = jnp.where(lax.broadcasted_iota(jnp.int32, (1, 2 * p), 1) < p, -1.0, 1.0)
    c_idx = lax.broadcasted_iota(jnp.int32, (n, 2 * p), 0) % chunks_per_seq
    d = 1
    while d < chunks_per_seq:
        xs = jnp.where(c_idx >= d, pltpu.roll(x, d, 0), 0.0)
        x = x + ar * xs + (sgn * ai) * pltpu.roll(xs, p, 1)
        ar, ai = ar * ar - ai * ai, 2.0 * ar * ai
        d *= 2
    x_prev = jnp.where(c_idx >= 1, pltpu.roll(x, 1, 0), 0.0)
    y = _dot(u, wbig_ref[0, 0:width, :]) + _dot(x_prev.astype(BF16), wbig_ref[0, width:, :])
    y_ref[0] = y + dexp_ref[0] * uf


def _ssm_kernel(chunks_per_seq, lsc_ref, lrc_ref, lic_ref, lsr_ref, lrr_ref, lir_ref, btr_ref, bti_ref,
                ctr_ref, cti_ref, u_ref, dexp_ref, y_ref, wbig_ref, w1_ref, lamt_ref):
    _ssm_operators(lsc_ref, lrc_ref, lic_ref, lsr_ref, lrr_ref, lir_ref, btr_ref, bti_ref,
                   ctr_ref, cti_ref, wbig_ref, w1_ref, lamt_ref)
    _ssm_apply(chunks_per_seq, u_ref, wbig_ref, w1_ref, lamt_ref, dexp_ref, y_ref)


def _ssm(u_g, log_step, lam_re, lam_im, b_re, b_im, c_re, c_im, d, chunks_per_seq):
    g, n, width = u_g.shape
    p, hh, t = SSM_STATE, SSM_GROUP, SSM_CHUNK
    col = lambda v: v[:, :, None]
    row2 = lambda v: jnp.concatenate([v, v], axis=-1)[:, None, :]
    ls = jnp.broadcast_to(log_step[:, None], (g, p))
    bt = lambda b: jnp.concatenate([b.transpose(0, 2, 1)] * 2, axis=-1)
    ct = lambda c: c.transpose(0, 2, 1)
    dexp = jnp.tile(d.reshape(g, hh), (1, t))[:, None, :]
    grp = lambda *s: pl.BlockSpec((1,) + s, lambda i: (i,) + (0,) * len(s))
    return pl.pallas_call(
        functools.partial(_ssm_kernel, chunks_per_seq),
        grid=(g,),
        in_specs=([grp(p, 1)] * 3 + [grp(1, 2 * p)] * 3 + [grp(hh, 2 * p)] * 2 + [grp(p, hh)] * 2
                  + [grp(n, width), grp(1, width)]),
        out_specs=grp(n, width),
        out_shape=jax.ShapeDtypeStruct((g, n, width), F32),
        scratch_shapes=[pltpu.VMEM((1, width + 2 * p, width), BF16),
                        pltpu.VMEM((1, width, 2 * p), BF16),
                        pltpu.VMEM((1, SUBLANES, 2 * p), F32)],
        compiler_params=_params(),
        name="ssm",
    )(col(ls), col(lam_re), col(lam_im), row2(ls), row2(lam_re), row2(lam_im),
      bt(b_re), bt(b_im), ct(c_re), ct(c_im), u_g, dexp)


def _merge_kernel(x_ref, ing_ref, inb_ref, yg_ref, rowperm_ref, perm_ref, ma_ref, gb_ref, wglu_ref,
                  wmix_ref, g_ref, b_ref, o_ref):
    t = SSM_CHUNK
    per_tile = LANES // SSM_GROUP
    n_q = D_SSM // LANES
    rows = []
    for m in range(t // per_tile):
        for q in range(n_q):
            rows.append(jnp.concatenate(
                [yg_ref[q * per_tile + gam, :, m * LANES:(m + 1) * LANES] for gam in range(per_tile)],
                axis=1))
    swapped = _dot(jnp.concatenate(rows, axis=0).astype(BF16), perm_ref[...])
    blocks = []
    for m in range(t // per_tile):
        for i0 in range(per_tile):
            blocks.append(jnp.concatenate(
                [swapped[(m * n_q + q) * SUBLANES:(m * n_q + q + 1) * SUBLANES, i0 * LANES:(i0 + 1) * LANES]
                 for q in range(n_q)], axis=1))
    y_kc = jnp.concatenate(blocks, axis=0).astype(BF16)
    ys = _dot(rowperm_ref[...], y_kc).astype(BF16)
    z = _dot(ys, wglu_ref[...])
    yb = z[:, :D_MODEL] * jax.nn.sigmoid(z[:, D_MODEL:])
    merged = ma_ref[...].astype(F32) + gb_ref[...].astype(F32) * yb
    mix = _dot(merged.astype(BF16), wmix_ref[...])
    h = _layer_norm(x_ref[...], ing_ref[...], inb_ref[...])
    o_ref[...] = _layer_norm(DEEPNORM_ALPHA * h + mix, g_ref[...], b_ref[...])


def _merge(x2, ing, inb, y_g, ma, gb, w_glu, w_mix, g, b):
    n = x2.shape[0]
    chunks = ROW_TILE // SSM_CHUNK
    assert chunks == SUBLANES, "one row tile must fill the sublanes of a regrouped S5 block"
    row = lambda w: pl.BlockSpec((ROW_TILE, w), lambda i: (i, 0))
    vec = _const_spec((1, D_MODEL))
    return pl.pallas_call(
        _merge_kernel,
        grid=(n // ROW_TILE,),
        in_specs=[row(D_MODEL), vec, vec,
                  pl.BlockSpec((SSM_GROUPS, chunks, SSM_GROUP * SSM_CHUNK), lambda i: (0, i, 0)),
                  _const_spec((ROW_TILE, ROW_TILE)), _const_spec((SUBLANES * LANES, SUBLANES * LANES)),
                  row(D_MODEL), row(D_MODEL),
                  _const_spec((D_SSM, 2 * D_MODEL)), _const_spec((D_MODEL, D_MODEL)), vec, vec],
        out_specs=row(D_MODEL),
        out_shape=jax.ShapeDtypeStruct((n, D_MODEL), F32),
        compiler_params=_params(),
        name="merge_ln1",
    )(x2, ing, inb, y_g, _chunk_transpose_matrix(ROW_TILE, chunks), _block_swap_matrix(), ma, gb,
      w_glu, w_mix, g, b)


def _kv_kernel(m_ref, w_ref, k_ref, v_ref):
    kv = _dot(m_ref[...].astype(BF16), w_ref[...])
    k_ref[...] = kv[:, :D_MODEL].astype(BF16)
    v_ref[...] = kv[:, D_MODEL:].astype(BF16)


def _kv(mem2, w_kv, n_mem):
    n = mem2.shape[0]
    row = lambda w: pl.BlockSpec((n_mem, w), lambda i: (i, 0))
    return pl.pallas_call(
        _kv_kernel,
        grid=(n // n_mem,),
        in_specs=[row(D_MODEL), _const_spec((D_MODEL, 2 * D_MODEL))],
        out_specs=[row(D_MODEL), row(D_MODEL)],
        out_shape=[jax.ShapeDtypeStruct((n, D_MODEL), BF16)] * 2,
        compiler_params=_params(),
        name="mem_kv",
    )(mem2, w_kv)


def _attn_kernel(h_ref, k_ref, v_ref, wq_ref, wo_ref, g_ref, b_ref, o_ref):
    h = h_ref[...]
    q = _dot(h.astype(BF16), wq_ref[...]).astype(BF16)
    heads = []
    for hd in range(XA_HEADS):
        sl = slice(hd * XA_HEAD_DIM, (hd + 1) * XA_HEAD_DIM)
        s = lax.dot_general(q[:, sl], k_ref[:, sl], (((1,), (1,)), ((), ())),
                            preferred_element_type=F32) * (XA_HEAD_DIM ** -0.5)
        e = jnp.exp(s - jnp.max(s, axis=-1, keepdims=True))
        pr = e / jnp.sum(e, axis=-1, keepdims=True)
        heads.append(_dot(pr.astype(BF16), v_ref[:, sl]).astype(BF16))
    xa = _dot(jnp.concatenate(heads, axis=1), wo_ref[...])
    o_ref[...] = _layer_norm(DEEPNORM_ALPHA * h + xa, g_ref[...], b_ref[...])


def _attn(h1, k, v, wq, wo, g, b, seq, n_mem):
    n = h1.shape[0]
    tiles_per_seq = seq // ROW_TILE
    row = pl.BlockSpec((ROW_TILE, D_MODEL), lambda i: (i, 0))
    memb = pl.BlockSpec((n_mem, D_MODEL), lambda i: (i // tiles_per_seq, 0))
    vec = _const_spec((1, D_MODEL))
    sq = _const_spec((D_MODEL, D_MODEL))
    return pl.pallas_call(
        _attn_kernel,
        grid=(n // ROW_TILE,),
        in_specs=[row, memb, memb, sq, sq, vec, vec],
        out_specs=row,
        out_shape=jax.ShapeDtypeStruct((n, D_MODEL), F32),
        compiler_params=_params(),
        name="mem_attn_ln2",
    )(h1, k, v, wq, wo, g, b)


def _mlp_kernel(h_ref, wu_ref, wd_ref, g_ref, b_ref, o_ref):
    h = h_ref[...]
    hb = h.astype(BF16)
    ff = jnp.zeros((ROW_TILE, D_MODEL), F32)
    for c in range(D_FF // D_MODEL):
        sl = slice(c * D_MODEL, (c + 1) * D_MODEL)
        z = jnp.maximum(_dot(hb, wu_ref[:, sl]), 0.0)
        ff = ff + _dot((z * z).astype(BF16), wd_ref[sl, :])
    o_ref[...] = _layer_norm(DEEPNORM_ALPHA * h + ff, g_ref[...], b_ref[...])


def _mlp(h2, w_up, w_down, g, b):
    n = h2.shape[0]
    row = pl.BlockSpec((ROW_TILE, D_MODEL), lambda i: (i, 0))
    vec = _const_spec((1, D_MODEL))
    return pl.pallas_call(
        _mlp_kernel,
        grid=(n // ROW_TILE,),
        in_specs=[row, _const_spec((D_MODEL, D_FF)), _const_spec((D_FF, D_MODEL)), vec, vec],
        out_specs=row,
        out_shape=jax.ShapeDtypeStruct((n, D_MODEL), F32),
        compiler_params=_params(),
        name="mlp_ln3",
    )(h2, w_up, w_down, g, b)


def _layer(h_in, mem2, ing, inb, w_in, conv_dw, conv_db, conv_ng, conv_nb, w_conv_out, log_step,
           lam_re, lam_im, b_re, b_im, c_re, c_im, d, w_ssm_glu, w_mix_out, ln1_g, ln1_b,
           wq, wkv, wo, ln2_g, ln2_b, w_up, w_down, ln3_g, ln3_b, seq, n_mem):
    vec = lambda v: v.reshape(1, -1)
    ma, u_g, gb = _in_conv(h_in, ing, inb, w_in.astype(BF16), conv_dw, vec(conv_db), vec(conv_ng),
                           vec(conv_nb), w_conv_out.astype(BF16), seq)
    y_g = _ssm(u_g, log_step, lam_re, lam_im, b_re, b_im, c_re, c_im, d, seq // SSM_CHUNK)
    h1 = _merge(h_in, ing, inb, y_g, ma, gb, w_ssm_glu.astype(BF16), w_mix_out.astype(BF16),
                vec(ln1_g), vec(ln1_b))
    k, v = _kv(mem2, wkv.astype(BF16), n_mem)
    h2 = _attn(h1, k, v, wq.astype(BF16), wo.astype(BF16), vec(ln2_g), vec(ln2_b), seq, n_mem)
    return _mlp(h2, w_up.astype(BF16), w_down.astype(BF16), vec(ln3_g), vec(ln3_b))


def kernel(x, mem, in_norm_g, in_norm_b, w_in, conv_dw, conv_db, conv_norm_g, conv_norm_b, w_conv_out, ssm_log_step, ssm_lambda_re, ssm_lambda_im, ssm_b_re, ssm_b_im, ssm_c_re, ssm_c_im, ssm_d, w_ssm_glu, w_mix_out, ln1_g, ln1_b, xa_wq, xa_wkv, xa_wo, ln2_g, ln2_b, mlp_w_up, mlp_w_down, ln3_g, ln3_b):
    bsz, seq, dm = x.shape
    n_mem = mem.shape[1]
    depth = w_in.shape[0]
    assert depth == 1, "the input normalisation is fused into the single layer's kernels"
    assert dm == D_MODEL and seq % ROW_TILE == 0 and seq % SSM_CHUNK == 0
    x2 = x.reshape(bsz * seq, dm)
    mem2 = mem.reshape(bsz * n_mem, dm)
    out = _layer(x2, mem2, in_norm_g.reshape(1, -1), in_norm_b.reshape(1, -1), w_in[0], conv_dw[0],
                 conv_db[0], conv_norm_g[0], conv_norm_b[0], w_conv_out[0], ssm_log_step[0],
                 ssm_lambda_re[0], ssm_lambda_im[0], ssm_b_re[0], ssm_b_im[0], ssm_c_re[0],
                 ssm_c_im[0], ssm_d[0], w_ssm_glu[0], w_mix_out[0], ln1_g[0], ln1_b[0], xa_wq[0],
                 xa_wkv[0], xa_wo[0], ln2_g[0], ln2_b[0], mlp_w_up[0], mlp_w_down[0], ln3_g[0],
                 ln3_b[0], seq, n_mem)
    return out.reshape(bsz, seq, dm)
```

```python
import functools

import jax
import jax.numpy as jnp
from jax import lax
from jax.experimental import pallas as pl
from jax.experimental.pallas import tpu as pltpu

F32 = jnp.float32
BF16 = jnp.bfloat16

D_MODEL = 1024
D_CONV = 1024
CONV_K = 31
D_SSM = 512
SSM_GROUP = 16
SSM_GROUPS = 32
SSM_STATE = 64
XA_HEADS = 4
XA_HEAD_DIM = 256
D_FF = 4096
LN_EPS = 1e-5
DEEPNORM_ALPHA = 2.0 ** 0.25

LANES = 128
SUBLANES = 8
MXU_WIDTH = 256
SSM_CHUNK = 64
CONV_HALO = 32
CONV_ROWS = 128
ROW_TILE = 512
SPLIT_ROWS = 256
VMEM_LIMIT = 56 * 1024 * 1024


def _layer_norm(x, g, b):
    mu = jnp.mean(x, axis=-1, keepdims=True)
    xc = x - mu
    var = jnp.mean(xc * xc, axis=-1, keepdims=True)
    return xc * lax.rsqrt(var + LN_EPS) * g + b


def _dot(a, b):
    return jnp.dot(a, b, preferred_element_type=F32)


def _params(semantics="parallel"):
    return pltpu.CompilerParams(dimension_semantics=(semantics,), vmem_limit_bytes=VMEM_LIMIT)


def _const_spec(shape):
    nd = len(shape)
    return pl.BlockSpec(shape, lambda *_: (0,) * nd)


def _block_swap_matrix():
    idx = jnp.arange(SUBLANES * LANES)
    a, b, h = idx // LANES, (idx % LANES) // SSM_GROUP, idx % SSM_GROUP
    return (idx[None, :] == (b * LANES + a * SSM_GROUP + h)[:, None]).astype(BF16)


def _chunk_transpose_matrix(rows, inner):
    idx = jnp.arange(rows)
    src = (idx % (rows // inner)) * inner + idx // (rows // inner)
    return (idx[None, :] == src[:, None]).astype(BF16)


def _in_conv_kernel(tiles_per_seq, x_ref, ing_ref, inb_ref, w_ref, rowperm_ref, perm_ref, dw_ref,
                    db_ref, ng_ref, nb_ref, wco_ref, ma_ref, ug_ref, gb_ref, hn_ref, ext_ref, raw_ref):
    tm = ROW_TILE
    n_col = D_CONV // LANES
    o0, o1, o2, o3 = D_CONV, 2 * D_CONV, 2 * D_CONV + D_SSM, 2 * D_CONV + D_SSM + D_MODEL
    first = (pl.program_id(0) % tiles_per_seq) == 0

    @pl.when(first)
    def _():
        ext_ref[:, 0:CONV_HALO, :] = jnp.zeros((n_col, CONV_HALO, LANES), F32)

    @pl.when(jnp.logical_not(first))
    def _():
        ext_ref[:, 0:CONV_HALO, :] = ext_ref[:, tm:tm + CONV_HALO, :]

    hn = _layer_norm(x_ref[...], ing_ref[...], inb_ref[...])
    hn_ref[...] = hn
    h = hn.astype(BF16)

    def zero_after(v):
        bits = pltpu.bitcast(v, jnp.uint32)
        return lax.shift_right_logical(lax.shift_right_logical(bits, jnp.uint32(16)), jnp.uint32(16))

    def weights(cols_w, zero):
        w = w_ref[:, cols_w]
        if zero is None:
            return w
        head = pltpu.bitcast(pltpu.bitcast(w[0:2 * SUBLANES, 0:LANES], jnp.uint32) | zero, BF16)
        top = jnp.concatenate([head, w[0:2 * SUBLANES, LANES:]], axis=1)
        return jnp.concatenate([top, w[2 * SUBLANES:]], axis=0)

    def glu_piece(p):
        def run(zero):
            gate = _dot(h, weights(slice(o0 + p * MXU_WIDTH, o0 + (p + 1) * MXU_WIDTH), zero))
            u = _dot(h, weights(slice(p * MXU_WIDTH, (p + 1) * MXU_WIDTH), zero)) * jax.nn.sigmoid(gate)
            for half in range(MXU_WIDTH // LANES):
                ext_ref[p * (MXU_WIDTH // LANES) + half, CONV_HALO:, :] = (
                    u[:, half * LANES:(half + 1) * LANES])
            return gate[0:SUBLANES, 0:LANES]
        return run

    def gate_pieces(i0):
        def run(zero):
            for i in (i0, i0 + 1):
                raw = _dot(h, weights(slice(o2 + i * MXU_WIDTH, o2 + (i + 1) * MXU_WIDTH), zero))
                raw_ref[:, i * MXU_WIDTH:(i + 1) * MXU_WIDTH] = raw
            return raw[0:SUBLANES, 0:LANES]
        return run

    def ssm_input(zero):
        t = SSM_CHUNK
        per_tile = LANES // SSM_GROUP
        n_q = D_SSM // LANES
        n_c = tm // t
        s_kc = _dot(rowperm_ref[...], _dot(h, weights(slice(o1, o2), zero)).astype(BF16))
        rows = []
        for m in range(t // per_tile):
            for q in range(n_q):
                rows.append(jnp.concatenate(
                    [s_kc[(per_tile * m + k0) * n_c:(per_tile * m + k0 + 1) * n_c, q * LANES:(q + 1) * LANES]
                     for k0 in range(per_tile)], axis=1))
        swapped = _dot(jnp.concatenate(rows, axis=0).astype(BF16), perm_ref[...])
        for m in range(t // per_tile):
            for q in range(n_q):
                r0 = (m * n_q + q) * SUBLANES
                for gam in range(per_tile):
                    ug_ref[q * per_tile + gam, :, m * LANES:(m + 1) * LANES] = (
                        swapped[r0:r0 + SUBLANES, gam * LANES:(gam + 1) * LANES])
        return swapped[0:SUBLANES, 0:LANES]

    base = CONV_HALO - (CONV_K - 1)

    def conv_block(c, after):
        cs = slice(c * LANES, (c + 1) * LANES)
        taps = dw_ref[:, cs]
        if after is not None:
            zero = zero_after(after)[0:1, :]
            taps = [pltpu.bitcast(pltpu.bitcast(taps[j:j + 1, :], jnp.uint32) | zero, F32)
                    for j in range(CONV_K)]
        else:
            taps = [taps[j:j + 1, :] for j in range(CONV_K)]
        blocks = []
        for r0 in range(0, tm, CONV_ROWS):
            acc = jnp.broadcast_to(db_ref[:, cs], (CONV_ROWS, LANES))
            for j in range(CONV_K):
                acc = acc + taps[j] * ext_ref[c, base + j + r0:base + j + r0 + CONV_ROWS, :]
            blocks.append(acc)
        return jnp.concatenate(blocks, axis=0)

    side = [glu_piece(1), gate_pieces(0), glu_piece(2), gate_pieces(2), glu_piece(3), gate_pieces(4),
            ssm_input, gate_pieces(6)]
    assert len(side) == n_col
    glu_piece(0)(None)
    cols = []
    side_done = []
    for c in range(n_col):
        side_done.append(side[c](zero_after(cols[c - 2][0:SUBLANES, :]) if c >= 2 else None))
        cols.append(conv_block(c, side_done[c - 1] if c >= 1 else None))
    conv = _layer_norm(jnp.concatenate(cols, axis=1), ng_ref[...], nb_ref[...])
    conv = conv * jax.nn.sigmoid(conv)
    ya = _dot(conv.astype(BF16), wco_ref[...])
    ma_ref[...] = (jax.nn.sigmoid(raw_ref[:, :D_MODEL]) * ya).astype(BF16)
    gb_ref[...] = jax.nn.sigmoid(raw_ref[:, D_MODEL:]).astype(BF16)


def _in_conv(x2, ing, inb, w_in, dw, db, ng, nb, w_conv_out, seq):
    n = x2.shape[0]
    tm = ROW_TILE
    d_in = w_in.shape[1]
    chunks = tm // SSM_CHUNK
    assert chunks == SUBLANES, "one row tile must fill the sublanes of a regrouped S5 block"
    row = lambda w: pl.BlockSpec((tm, w), lambda i: (i, 0))
    vec = _const_spec((1, D_MODEL))
    return pl.pallas_call(
        functools.partial(_in_conv_kernel, seq // tm),
        grid=(n // tm,),
        in_specs=[row(D_MODEL), vec, vec, _const_spec((D_MODEL, d_in)),
                  _const_spec((tm, tm)), _const_spec((SUBLANES * LANES, SUBLANES * LANES)),
                  _const_spec((CONV_K, D_CONV)), vec, vec, vec, _const_spec((D_CONV, D_MODEL))],
        out_specs=[row(D_MODEL),
                   pl.BlockSpec((SSM_GROUPS, chunks, SSM_GROUP * SSM_CHUNK), lambda i: (0, i, 0)),
                   row(D_MODEL), row(D_MODEL)],
        out_shape=[jax.ShapeDtypeStruct((n, D_MODEL), BF16),
                   jax.ShapeDtypeStruct((SSM_GROUPS, n // SSM_CHUNK, SSM_GROUP * SSM_CHUNK), F32),
                   jax.ShapeDtypeStruct((n, D_MODEL), BF16),
                   jax.ShapeDtypeStruct((n, D_MODEL), F32)],
        scratch_shapes=[pltpu.VMEM((D_CONV // LANES, CONV_HALO + tm, LANES), F32),
                        pltpu.VMEM((tm, 2 * D_MODEL), F32)],
        compiler_params=_params("arbitrary"),
        name="in_conv",
    )(x2, ing, inb, w_in, _chunk_transpose_matrix(tm, SSM_CHUNK), _block_swap_matrix(), dw, db, ng, nb,
      w_conv_out)


def _ssm_operators(lsr_ref, lrr_ref, lir_ref, btr_ref, bti_ref, ctr_ref, cti_ref,
                   wbig_ref, w1_ref, lamt_ref):
    t = SSM_CHUNK
    p = SSM_STATE
    hh = SSM_GROUP
    width = hh * t
    assert t == SUBLANES * SUBLANES
    hp = lax.Precision.HIGHEST

    step_r = jnp.exp(lsr_ref[0])
    lr = lrr_ref[0]
    li = lir_ref[0]
    a_r = lr * step_r
    th_r = li * step_r

    def powers(k):
        mag = jnp.exp(a_r * k)
        return mag * jnp.cos(th_r * k), mag * jnp.sin(th_r * k)

    sub = lax.broadcasted_iota(jnp.int32, (SUBLANES, 2 * p), 0).astype(F32)
    fine_r, fine_i = powers(sub)
    coarse_r, coarse_i = powers(sub * float(SUBLANES))
    rows_r, rows_i = [], []
    for a in range(t // SUBLANES):
        cr8 = coarse_r[a:a + 1, :]
        ci8 = coarse_i[a:a + 1, :]
        rows_r.append(cr8 * fine_r - ci8 * fine_i)
        rows_i.append(cr8 * fine_i + ci8 * fine_r)
    lkr = jnp.concatenate(rows_r, axis=0)
    lki = jnp.concatenate(rows_i, axis=0)

    ar = lkr[1:2, :]
    ai = lki[1:2, :]
    den = lr * lr + li * li
    nr = ar - 1.0
    cr = (nr * lr + ai * li) / den
    ci = (ai * lr - nr * li) / den
    btr = btr_ref[0]
    bti = bti_ref[0]
    bbr = cr * btr - ci * bti
    bbi = cr * bti + ci * btr

    def widen(x, onehot):
        hi = x.astype(BF16)
        rest = x - hi.astype(F32)
        mid = rest.astype(BF16)
        lo = (rest - mid.astype(F32)).astype(BF16)
        return _dot(hi, onehot) + _dot(mid, onehot) + _dot(lo, onehot)

    l0r = lkr.T[0:p, :]
    l0i = lki.T[0:p, :]
    lam_r = l0r[:, 1:2]
    lam_i = l0i[:, 1:2]
    col_t = lax.broadcasted_iota(jnp.int32, (t, width), 1)
    by_step = (lax.broadcasted_iota(jnp.int32, (t, width), 0) == col_t // hh).astype(BF16)
    col_h = lax.broadcasted_iota(jnp.int32, (hh, width), 1)
    by_chan = (lax.broadcasted_iota(jnp.int32, (hh, width), 0) == col_h % hh).astype(BF16)
    l0r = widen(l0r, by_step)
    l0i = widen(l0i, by_step)
    cer = widen(ctr_ref[0], by_chan)
    cei = widen(cti_ref[0], by_chan)
    cl0r = cer * l0r - cei * l0i
    cl0i = cer * l0i + cei * l0r
    cl1r = cl0r * lam_r - cl0i * lam_i
    cl1i = cl0r * lam_i + cl0i * lam_r
    wbig_ref[0, width:width + p, :] = cl1r.astype(BF16)
    wbig_ref[0, width + p:width + 2 * p, :] = (-cl1i).astype(BF16)

    g2 = (jnp.dot(bbr[:, :p], cl0r, precision=hp, preferred_element_type=F32)
          - jnp.dot(bbi[:, :p], cl0i, precision=hp, preferred_element_type=F32))
    col = lax.broadcasted_iota(jnp.int32, (hh, width), 1)
    for k in range(t):
        taps = g2 if k == 0 else pltpu.roll(g2, k * hh, 1)
        wbig_ref[0, k * hh:(k + 1) * hh, :] = jnp.where(col >= k * hh, taps, 0.0).astype(BF16)

    real_half = lax.broadcasted_iota(jnp.int32, (hh, 2 * p), 1) < p
    for k in range(t):
        er = lkr[t - 1 - k:t - k, :]
        ei = lki[t - 1 - k:t - k, :]
        w1_ref[0, k * hh:(k + 1) * hh, :] = jnp.where(real_half, er * bbr - ei * bbi,
                                                      er * bbi + ei * bbr).astype(BF16)

    ltr, lti = powers(float(t))
    row = lax.broadcasted_iota(jnp.int32, (SUBLANES, 2 * p), 0)
    lamt_ref[0] = jnp.where(row == 0, ltr, jnp.where(row == 1, lti, 0.0))


def _ssm_apply(chunks_per_seq, u_ref, wbig_ref, w1_ref, lamt_ref, dexp_ref, y_ref):
    p = SSM_STATE
    width = SSM_GROUP * SSM_CHUNK
    uf = u_ref[0]
    u = uf.astype(BF16)
    n = u.shape[0]
    x = _dot(u, w1_ref[0])
    lam = lamt_ref[0]
    ar = lam[0:1, :]
    ai = lam[1:2, :]
    sgn = jnp.where(lax.broadcasted_iota(jnp.int32, (1, 2 * p), 1) < p, -1.0, 1.0)
    c_idx = lax.broadcasted_iota(jnp.int32, (n, 2 * p), 0) % chunks_per_seq
    d = 1
    while d < chunks_per_seq:
        xs = jnp.where(c_idx >= d, pltpu.roll(x, d, 0), 0.0)
        x = x + ar * xs + (sgn * ai) * pltpu.roll(xs, p, 1)
        ar, ai = ar * ar - ai * ai, 2.0 * ar * ai
        d *= 2
    x_prev = jnp.where(c_idx >= 1, pltpu.roll(x, 1, 0), 0.0)
    y = _dot(u, wbig_ref[0, 0:width, :]) + _dot(x_prev.astype(BF16), wbig_ref[0, width:, :])
    y_ref[0] = y + dexp_ref[0] * uf


def _ssm_kernel(chunks_per_seq, lsr_ref, lrr_ref, lir_ref, btr_ref, bti_ref, ctr_ref, cti_ref,
                u_ref, dexp_ref, y_ref, wbig_ref, w1_ref, lamt_ref):
    _ssm_operators(lsr_ref, lrr_ref, lir_ref, btr_ref, bti_ref, ctr_ref, cti_ref,
                   wbig_ref, w1_ref, lamt_ref)
    _ssm_apply(chunks_per_seq, u_ref, wbig_ref, w1_ref, lamt_ref, dexp_ref, y_ref)


def _ssm(u_g, log_step, lam_re, lam_im, b_re, b_im, c_re, c_im, d, chunks_per_seq):
    g, n, width = u_g.shape
    p, hh, t = SSM_STATE, SSM_GROUP, SSM_CHUNK
    row2 = lambda v: jnp.concatenate([v, v], axis=-1)[:, None, :]
    ls = jnp.broadcast_to(log_step[:, None], (g, p))
    bt = lambda b: jnp.concatenate([b.transpose(0, 2, 1)] * 2, axis=-1)
    ct = lambda c: c.transpose(0, 2, 1)
    dexp = jnp.tile(d.reshape(g, hh), (1, t))[:, None, :]
    grp = lambda *s: pl.BlockSpec((1,) + s, lambda i: (i,) + (0,) * len(s))
    return pl.pallas_call(
        functools.partial(_ssm_kernel, chunks_per_seq),
        grid=(g,),
        in_specs=([grp(1, 2 * p)] * 3 + [grp(hh, 2 * p)] * 2 + [grp(p, hh)] * 2
                  + [grp(n, width), grp(1, width)]),
        out_specs=grp(n, width),
        out_shape=jax.ShapeDtypeStruct((g, n, width), F32),
        scratch_shapes=[pltpu.VMEM((1, width + 2 * p, width), BF16),
                        pltpu.VMEM((1, width, 2 * p), BF16),
                        pltpu.VMEM((1, SUBLANES, 2 * p), F32)],
        compiler_params=_params(),
        name="ssm",
    )(row2(ls), row2(lam_re), row2(lam_im), bt(b_re), bt(b_im), ct(c_re), ct(c_im), u_g, dexp)


def _merge_kernel(hn_ref, yg_ref, rowperm_ref, perm_ref, ma_ref, gb_ref, wglu_ref,
                  wmix_ref, g_ref, b_ref, o_ref):
    t = SSM_CHUNK
    per_tile = LANES // SSM_GROUP
    n_q = D_SSM // LANES
    rows = []
    for m in range(t // per_tile):
        for q in range(n_q):
            rows.append(jnp.concatenate(
                [yg_ref[q * per_tile + gam, :, m * LANES:(m + 1) * LANES] for gam in range(per_tile)],
                axis=1))
    swapped = _dot(jnp.concatenate(rows, axis=0).astype(BF16), perm_ref[...])
    blocks = []
    for m in range(t // per_tile):
        for i0 in range(per_tile):
            blocks.append(jnp.concatenate(
                [swapped[(m * n_q + q) * SUBLANES:(m * n_q + q + 1) * SUBLANES, i0 * LANES:(i0 + 1) * LANES]
                 for q in range(n_q)], axis=1))
    y_kc = jnp.concatenate(blocks, axis=0).astype(BF16)
    ys = _dot(rowperm_ref[...], y_kc).astype(BF16)
    for r0 in range(0, ROW_TILE, SPLIT_ROWS):
        rs = slice(r0, r0 + SPLIT_ROWS)
        z = _dot(ys[rs], wglu_ref[...])
        yb = z[:, :D_MODEL] * jax.nn.sigmoid(z[:, D_MODEL:])
        merged = ma_ref[rs, :].astype(F32) + gb_ref[rs, :].astype(F32) * yb
        mix = _dot(merged.astype(BF16), wmix_ref[...])
        o_ref[rs, :] = _layer_norm(DEEPNORM_ALPHA * hn_ref[rs, :] + mix, g_ref[...], b_ref[...])


def _merge(hn, y_g, ma, gb, w_glu, w_mix, g, b):
    n = hn.shape[0]
    chunks = ROW_TILE // SSM_CHUNK
    assert chunks == SUBLANES, "one row tile must fill the sublanes of a regrouped S5 block"
    row = lambda w: pl.BlockSpec((ROW_TILE, w), lambda i: (i, 0))
    vec = _const_spec((1, D_MODEL))
    return pl.pallas_call(
        _merge_kernel,
        grid=(n // ROW_TILE,),
        in_specs=[row(D_MODEL),
                  pl.BlockSpec((SSM_GROUPS, chunks, SSM_GROUP * SSM_CHUNK), lambda i: (0, i, 0)),
                  _const_spec((ROW_TILE, ROW_TILE)), _const_spec((SUBLANES * LANES, SUBLANES * LANES)),
                  row(D_MODEL), row(D_MODEL),
                  _const_spec((D_SSM, 2 * D_MODEL)), _const_spec((D_MODEL, D_MODEL)), vec, vec],
        out_specs=row(D_MODEL),
        out_shape=jax.ShapeDtypeStruct((n, D_MODEL), F32),
        compiler_params=_params(),
        name="merge_ln1",
    )(hn, y_g, _chunk_transpose_matrix(ROW_TILE, chunks), _block_swap_matrix(), ma, gb,
      w_glu, w_mix, g, b)


def _kv_kernel(m_ref, w_ref, k_ref, v_ref):
    kv = _dot(m_ref[...].astype(BF16), w_ref[...])
    k_ref[...] = kv[:, :D_MODEL].astype(BF16)
    v_ref[...] = kv[:, D_MODEL:].astype(BF16)


def _kv(mem2, w_kv, n_mem):
    n = mem2.shape[0]
    row = lambda w: pl.BlockSpec((n_mem, w), lambda i: (i, 0))
    return pl.pallas_call(
        _kv_kernel,
        grid=(n // n_mem,),
        in_specs=[row(D_MODEL), _const_spec((D_MODEL, 2 * D_MODEL))],
        out_specs=[row(D_MODEL), row(D_MODEL)],
        out_shape=[jax.ShapeDtypeStruct((n, D_MODEL), BF16)] * 2,
        compiler_params=_params(),
        name="mem_kv",
    )(mem2, w_kv)


def _attn_kernel(h_ref, k_ref, v_ref, wq_ref, wo_ref, g_ref, b_ref, o_ref):
    h = h_ref[...]
    q = _dot(h.astype(BF16), wq_ref[...]).astype(BF16)
    heads = []
    for hd in range(XA_HEADS):
        sl = slice(hd * XA_HEAD_DIM, (hd + 1) * XA_HEAD_DIM)
        s = lax.dot_general(q[:, sl], k_ref[:, sl], (((1,), (1,)), ((), ())),
                            preferred_element_type=F32) * (XA_HEAD_DIM ** -0.5)
        e = jnp.exp(s - jnp.max(s, axis=-1, keepdims=True))
        pr = e / jnp.sum(e, axis=-1, keepdims=True)
        heads.append(_dot(pr.astype(BF16), v_ref[:, sl]).astype(BF16))
    xa = _dot(jnp.concatenate(heads, axis=1), wo_ref[...])
    o_ref[...] = _layer_norm(DEEPNORM_ALPHA * h + xa, g_ref[...], b_ref[...])


def _attn(h1, k, v, wq, wo, g, b, seq, n_mem):
    n = h1.shape[0]
    tiles_per_seq = seq // ROW_TILE
    row = pl.BlockSpec((ROW_TILE, D_MODEL), lambda i: (i, 0))
    memb = pl.BlockSpec((n_mem, D_MODEL), lambda i: (i // tiles_per_seq, 0))
    vec = _const_spec((1, D_MODEL))
    sq = _const_spec((D_MODEL, D_MODEL))
    return pl.pallas_call(
        _attn_kernel,
        grid=(n // ROW_TILE,),
        in_specs=[row, memb, memb, sq, sq, vec, vec],
        out_specs=row,
        out_shape=jax.ShapeDtypeStruct((n, D_MODEL), F32),
        compiler_params=_params(),
        name="mem_attn_ln2",
    )(h1, k, v, wq, wo, g, b)


def _mlp_kernel(h_ref, wu_ref, wd_ref, g_ref, b_ref, o_ref):
    h = h_ref[...]
    hb = h.astype(BF16)
    ff = jnp.zeros((ROW_TILE, D_MODEL), F32)
    for c in range(D_FF // D_MODEL):
        sl = slice(c * D_MODEL, (c + 1) * D_MODEL)
        z = jnp.maximum(_dot(hb, wu_ref[:, sl]), 0.0)
        ff = ff + _dot((z * z).astype(BF16), wd_ref[sl, :])
    o_ref[...] = _layer_norm(DEEPNORM_ALPHA * h + ff, g_ref[...], b_ref[...])


def _mlp(h2, w_up, w_down, g, b):
    n = h2.shape[0]
    row = pl.BlockSpec((ROW_TILE, D_MODEL), lambda i: (i, 0))
    vec = _const_spec((1, D_MODEL))
    return pl.pallas_call(
        _mlp_kernel,
        grid=(n // ROW_TILE,),
        in_specs=[row, _const_spec((D_MODEL, D_FF)), _const_spec((D_FF, D_MODEL)), vec, vec],
        out_specs=row,
        out_shape=jax.ShapeDtypeStruct((n, D_MODEL), F32),
        compiler_params=_params(),
        name="mlp_ln3",
    )(h2, w_up, w_down, g, b)


def _layer(h_in, mem2, ing, inb, w_in, conv_dw, conv_db, conv_ng, conv_nb, w_conv_out, log_step,
           lam_re, lam_im, b_re, b_im, c_re, c_im, d, w_ssm_glu, w_mix_out, ln1_g, ln1_b,
           wq, wkv, wo, ln2_g, ln2_b, w_up, w_down, ln3_g, ln3_b, seq, n_mem):
    vec = lambda v: v.reshape(1, -1)
    ma, u_g, gb, hn = _in_conv(h_in, ing, inb, w_in.astype(BF16), conv_dw, vec(conv_db), vec(conv_ng),
                               vec(conv_nb), w_conv_out.astype(BF16), seq)
    y_g = _ssm(u_g, log_step, lam_re, lam_im, b_re, b_im, c_re, c_im, d, seq // SSM_CHUNK)
    h1 = _merge(hn, y_g, ma, gb, w_ssm_glu.astype(BF16), w_mix_out.astype(BF16),
                vec(ln1_g), vec(ln1_b))
    k, v = _kv(mem2, wkv.astype(BF16), n_mem)
    h2 = _attn(h1, k, v, wq.astype(BF16), wo.astype(BF16), vec(ln2_g), vec(ln2_b), seq, n_mem)
    return _mlp(h2, w_up.astype(BF16), w_down.astype(BF16), vec(ln3_g), vec(ln3_b))


def kernel(x, mem, in_norm_g, in_norm_b, w_in, conv_dw, conv_db, conv_norm_g, conv_norm_b, w_conv_out, ssm_log_step, ssm_lambda_re, ssm_lambda_im, ssm_b_re, ssm_b_im, ssm_c_re, ssm_c_im, ssm_d, w_ssm_glu, w_mix_out, ln1_g, ln1_b, xa_wq, xa_wkv, xa_wo, ln2_g, ln2_b, mlp_w_up, mlp_w_down, ln3_g, ln3_b):
    bsz, seq, dm = x.shape
    n_mem = mem.shape[1]
    depth = w_in.shape[0]
    assert depth == 1, "the input normalisation is fused into the single layer's kernels"
    assert dm == D_MODEL and seq % ROW_TILE == 0 and seq % SSM_CHUNK == 0
    x2 = x.reshape(bsz * seq, dm)
    mem2 = mem.reshape(bsz * n_mem, dm)
    out = _layer(x2, mem2, in_norm_g.reshape(1, -1), in_norm_b.reshape(1, -1), w_in[0], conv_dw[0],
                 conv_db[0], conv_norm_g[0], conv_norm_b[0], w_conv_out[0], ssm_log_step[0],
                 ssm_lambda_re[0], ssm_lambda_im[0], ssm_b_re[0], ssm_b_im[0], ssm_c_re[0],
                 ssm_c_im[0], ssm_d[0], w_ssm_glu[0], w_mix_out[0], ln1_g[0], ln1_b[0], xa_wq[0],
                 xa_wkv[0], xa_wo[0], ln2_g[0], ln2_b[0], mlp_w_up[0], mlp_w_down[0], ln3_g[0],
                 ln3_b[0], seq, n_mem)
    return out.reshape(bsz, seq, dm)
```

```python
import functools

import jax
import jax.numpy as jnp
from jax import lax
from jax.experimental import pallas as pl
from jax.experimental.pallas import tpu as pltpu

F32 = jnp.float32
BF16 = jnp.bfloat16

D_MODEL = 1024
D_CONV = 1024
CONV_K = 31
D_SSM = 512
SSM_GROUP = 16
SSM_GROUPS = 32
SSM_STATE = 64
XA_HEADS = 4
XA_HEAD_DIM = 256
D_FF = 4096
LN_EPS = 1e-5
DEEPNORM_ALPHA = 2.0 ** 0.25

LANES = 128
SUBLANES = 8
MXU_WIDTH = 256
SSM_CHUNK = 64
SSM_GROUPS_PER_STEP = 4
CONV_HALO = 32
CONV_ROWS = 128
ROW_TILE = 512
SPLIT_ROWS = 256
VMEM_LIMIT = 56 * 1024 * 1024


def _layer_norm(x, g, b):
    mu = jnp.mean(x, axis=-1, keepdims=True)
    xc = x - mu
    var = jnp.mean(xc * xc, axis=-1, keepdims=True)
    return xc * lax.rsqrt(var + LN_EPS) * g + b


def _dot(a, b):
    return jnp.dot(a, b, preferred_element_type=F32)


def _params(semantics="parallel"):
    return pltpu.CompilerParams(dimension_semantics=(semantics,), vmem_limit_bytes=VMEM_LIMIT)


def _const_spec(shape):
    nd = len(shape)
    return pl.BlockSpec(shape, lambda *_: (0,) * nd)


def _block_swap_matrix():
    idx = jnp.arange(SUBLANES * LANES)
    a, b, h = idx // LANES, (idx % LANES) // SSM_GROUP, idx % SSM_GROUP
    return (idx[None, :] == (b * LANES + a * SSM_GROUP + h)[:, None]).astype(BF16)


def _chunk_transpose_matrix(rows, inner):
    idx = jnp.arange(rows)
    src = (idx % (rows // inner)) * inner + idx // (rows // inner)
    return (idx[None, :] == src[:, None]).astype(BF16)


def _in_conv_kernel(tiles_per_seq, x_ref, ing_ref, inb_ref, w_ref, rowperm_ref, perm_ref, dw_ref,
                    db_ref, ng_ref, nb_ref, wco_ref, ma_ref, ug_ref, gb_ref, hn_ref, raw_ref, *ext_ref):
    tm = ROW_TILE
    n_col = D_CONV // LANES
    o0, o1, o2, o3 = D_CONV, 2 * D_CONV, 2 * D_CONV + D_SSM, 2 * D_CONV + D_SSM + D_MODEL
    first = (pl.program_id(0) % tiles_per_seq) == 0

    @pl.when(first)
    def _():
        for ref in ext_ref:
            ref[0:CONV_HALO, :] = jnp.zeros((CONV_HALO, LANES), F32)

    @pl.when(jnp.logical_not(first))
    def _():
        for ref in ext_ref:
            ref[0:CONV_HALO, :] = ref[tm:tm + CONV_HALO, :]

    hn = _layer_norm(x_ref[...], ing_ref[...], inb_ref[...])
    hn_ref[...] = hn
    h = hn.astype(BF16)

    def zero_after(v):
        bits = pltpu.bitcast(v, jnp.uint32)
        return lax.shift_right_logical(lax.shift_right_logical(bits, jnp.uint32(16)), jnp.uint32(16))

    def weights(cols_w, zero):
        w = w_ref[:, cols_w]
        if zero is None:
            return w
        head = pltpu.bitcast(pltpu.bitcast(w[0:2 * SUBLANES, 0:LANES], jnp.uint32) | zero, BF16)
        top = jnp.concatenate([head, w[0:2 * SUBLANES, LANES:]], axis=1)
        return jnp.concatenate([top, w[2 * SUBLANES:]], axis=0)

    def glu_piece(p):
        def run(zero):
            gate = _dot(h, weights(slice(o0 + p * MXU_WIDTH, o0 + (p + 1) * MXU_WIDTH), zero))
            u = _dot(h, weights(slice(p * MXU_WIDTH, (p + 1) * MXU_WIDTH), zero)) * jax.nn.sigmoid(gate)
            for half in range(MXU_WIDTH // LANES):
                ext_ref[p * (MXU_WIDTH // LANES) + half][CONV_HALO:, :] = (
                    u[:, half * LANES:(half + 1) * LANES])
            return gate[0:SUBLANES, 0:LANES]
        return run

    def gate_pieces(i0):
        def run(zero):
            for i in (i0, i0 + 1):
                raw = _dot(h, weights(slice(o2 + i * MXU_WIDTH, o2 + (i + 1) * MXU_WIDTH), zero))
                raw_ref[:, i * MXU_WIDTH:(i + 1) * MXU_WIDTH] = raw
            return raw[0:SUBLANES, 0:LANES]
        return run

    def ssm_input(zero):
        t = SSM_CHUNK
        per_tile = LANES // SSM_GROUP
        n_q = D_SSM // LANES
        n_c = tm // t
        s_kc = _dot(rowperm_ref[...], _dot(h, weights(slice(o1, o2), zero)).astype(BF16))
        rows = []
        for m in range(t // per_tile):
            for q in range(n_q):
                rows.append(jnp.concatenate(
                    [s_kc[(per_tile * m + k0) * n_c:(per_tile * m + k0 + 1) * n_c, q * LANES:(q + 1) * LANES]
                     for k0 in range(per_tile)], axis=1))
        swapped = _dot(jnp.concatenate(rows, axis=0).astype(BF16), perm_ref[...])
        for m in range(t // per_tile):
            for q in range(n_q):
                r0 = (m * n_q + q) * SUBLANES
                for gam in range(per_tile):
                    ug_ref[q * per_tile + gam, :, m * LANES:(m + 1) * LANES] = (
                        swapped[r0:r0 + SUBLANES, gam * LANES:(gam + 1) * LANES])
        return swapped[0:SUBLANES, 0:LANES]

    base = CONV_HALO - (CONV_K - 1)

    def conv_block(c, after):
        cs = slice(c * LANES, (c + 1) * LANES)
        taps = dw_ref[:, cs]
        if after is not None:
            zero = zero_after(after)[0:1, :]
            taps = [pltpu.bitcast(pltpu.bitcast(taps[j:j + 1, :], jnp.uint32) | zero, F32)
                    for j in range(CONV_K)]
        else:
            taps = [taps[j:j + 1, :] for j in range(CONV_K)]
        blocks = []
        for r0 in range(0, tm, CONV_ROWS):
            acc = jnp.broadcast_to(db_ref[:, cs], (CONV_ROWS, LANES))
            for j in range(CONV_K):
                acc = acc + taps[j] * ext_ref[c][base + j + r0:base + j + r0 + CONV_ROWS, :]
            blocks.append(acc)
        return jnp.concatenate(blocks, axis=0)

    side = [glu_piece(1), gate_pieces(0), glu_piece(2), gate_pieces(2), glu_piece(3), gate_pieces(4),
            ssm_input, gate_pieces(6)]
    assert len(side) == n_col
    glu_piece(0)(None)
    cols = []
    side_done = []
    for c in range(n_col):
        side_done.append(side[c](zero_after(cols[c - 2][0:SUBLANES, :]) if c >= 2 else None))
        cols.append(conv_block(c, side_done[c - 1] if c >= 1 else None))
    conv = _layer_norm(jnp.concatenate(cols, axis=1), ng_ref[...], nb_ref[...])
    conv = conv * jax.nn.sigmoid(conv)
    ya = _dot(conv.astype(BF16), wco_ref[...])
    ma_ref[...] = (jax.nn.sigmoid(raw_ref[:, :D_MODEL]) * ya).astype(BF16)
    gb_ref[...] = jax.nn.sigmoid(raw_ref[:, D_MODEL:]).astype(BF16)


def _in_conv(x2, ing, inb, w_in, dw, db, ng, nb, w_conv_out, seq):
    n = x2.shape[0]
    tm = ROW_TILE
    d_in = w_in.shape[1]
    chunks = tm // SSM_CHUNK
    assert chunks == SUBLANES, "one row tile must fill the sublanes of a regrouped S5 block"
    row = lambda w: pl.BlockSpec((tm, w), lambda i: (i, 0))
    vec = _const_spec((1, D_MODEL))
    return pl.pallas_call(
        functools.partial(_in_conv_kernel, seq // tm),
        grid=(n // tm,),
        in_specs=[row(D_MODEL), vec, vec, _const_spec((D_MODEL, d_in)),
                  _const_spec((tm, tm)), _const_spec((SUBLANES * LANES, SUBLANES * LANES)),
                  _const_spec((CONV_K, D_CONV)), vec, vec, vec, _const_spec((D_CONV, D_MODEL))],
        out_specs=[row(D_MODEL),
                   pl.BlockSpec((SSM_GROUPS, chunks, SSM_GROUP * SSM_CHUNK), lambda i: (0, i, 0)),
                   row(D_MODEL), row(D_MODEL)],
        out_shape=[jax.ShapeDtypeStruct((n, D_MODEL), BF16),
                   jax.ShapeDtypeStruct((SSM_GROUPS, n // SSM_CHUNK, SSM_GROUP * SSM_CHUNK), F32),
                   jax.ShapeDtypeStruct((n, D_MODEL), BF16),
                   jax.ShapeDtypeStruct((n, D_MODEL), F32)],
        scratch_shapes=([pltpu.VMEM((tm, 2 * D_MODEL), F32)]
                        + [pltpu.VMEM((CONV_HALO + tm, LANES), F32)] * (D_CONV // LANES)),
        compiler_params=_params("arbitrary"),
        name="in_conv",
    )(x2, ing, inb, w_in, _chunk_transpose_matrix(tm, SSM_CHUNK), _block_swap_matrix(), dw, db, ng, nb,
      w_conv_out)


def _ssm_operators(gi, lsr_ref, lrr_ref, lir_ref, btr_ref, bti_ref, ctr_ref, cti_ref,
                   wbig_ref, w1_ref, lamt_ref):
    t = SSM_CHUNK
    p = SSM_STATE
    hh = SSM_GROUP
    width = hh * t
    assert t == SUBLANES * SUBLANES
    hp = lax.Precision.HIGHEST

    step_r = jnp.exp(lsr_ref[gi])
    lr = lrr_ref[gi]
    li = lir_ref[gi]
    a_r = lr * step_r
    th_r = li * step_r

    def powers(k):
        mag = jnp.exp(a_r * k)
        return mag * jnp.cos(th_r * k), mag * jnp.sin(th_r * k)

    sub = lax.broadcasted_iota(jnp.int32, (SUBLANES, 2 * p), 0).astype(F32)
    fine_r, fine_i = powers(sub)
    coarse_r, coarse_i = powers(sub * float(SUBLANES))
    rows_r, rows_i = [], []
    for a in range(t // SUBLANES):
        cr8 = coarse_r[a:a + 1, :]
        ci8 = coarse_i[a:a + 1, :]
        rows_r.append(cr8 * fine_r - ci8 * fine_i)
        rows_i.append(cr8 * fine_i + ci8 * fine_r)
    lkr = jnp.concatenate(rows_r, axis=0)
    lki = jnp.concatenate(rows_i, axis=0)

    ar = lkr[1:2, :]
    ai = lki[1:2, :]
    den = lr * lr + li * li
    nr = ar - 1.0
    cr = (nr * lr + ai * li) / den
    ci = (ai * lr - nr * li) / den
    btr = btr_ref[gi]
    bti = bti_ref[gi]
    bbr = cr * btr - ci * bti
    bbi = cr * bti + ci * btr

    def widen(x, onehot):
        hi = x.astype(BF16)
        rest = x - hi.astype(F32)
        mid = rest.astype(BF16)
        lo = (rest - mid.astype(F32)).astype(BF16)
        return _dot(hi, onehot) + _dot(mid, onehot) + _dot(lo, onehot)

    l0r = lkr.T[0:p, :]
    l0i = lki.T[0:p, :]
    lam_r = l0r[:, 1:2]
    lam_i = l0i[:, 1:2]
    col_t = lax.broadcasted_iota(jnp.int32, (t, width), 1)
    by_step = (lax.broadcasted_iota(jnp.int32, (t, width), 0) == col_t // hh).astype(BF16)
    col_h = lax.broadcasted_iota(jnp.int32, (hh, width), 1)
    by_chan = (lax.broadcasted_iota(jnp.int32, (hh, width), 0) == col_h % hh).astype(BF16)
    l0r = widen(l0r, by_step)
    l0i = widen(l0i, by_step)
    cer = widen(ctr_ref[gi], by_chan)
    cei = widen(cti_ref[gi], by_chan)
    cl0r = cer * l0r - cei * l0i
    cl0i = cer * l0i + cei * l0r
    cl1r = cl0r * lam_r - cl0i * lam_i
    cl1i = cl0r * lam_i + cl0i * lam_r
    wbig_ref[gi, width:width + p, :] = cl1r.astype(BF16)
    wbig_ref[gi, width + p:width + 2 * p, :] = (-cl1i).astype(BF16)

    g2 = (jnp.dot(bbr[:, :p], cl0r, precision=hp, preferred_element_type=F32)
          - jnp.dot(bbi[:, :p], cl0i, precision=hp, preferred_element_type=F32))
    col = lax.broadcasted_iota(jnp.int32, (hh, width), 1)
    for k in range(t):
        taps = g2 if k == 0 else pltpu.roll(g2, k * hh, 1)
        wbig_ref[gi, k * hh:(k + 1) * hh, :] = jnp.where(col >= k * hh, taps, 0.0).astype(BF16)

    real_half = lax.broadcasted_iota(jnp.int32, (hh, 2 * p), 1) < p
    for k in range(t):
        er = lkr[t - 1 - k:t - k, :]
        ei = lki[t - 1 - k:t - k, :]
        w1_ref[gi, k * hh:(k + 1) * hh, :] = jnp.where(real_half, er * bbr - ei * bbi,
                                                      er * bbi + ei * bbr).astype(BF16)

    ltr, lti = powers(float(t))
    row = lax.broadcasted_iota(jnp.int32, (SUBLANES, 2 * p), 0)
    lamt_ref[gi] = jnp.where(row == 0, ltr, jnp.where(row == 1, lti, 0.0))


def _ssm_apply(gi, chunks_per_seq, u_ref, wbig_ref, w1_ref, lamt_ref, dexp_ref, y_ref):
    p = SSM_STATE
    width = SSM_GROUP * SSM_CHUNK
    uf = u_ref[gi]
    u = uf.astype(BF16)
    n = u.shape[0]
    x = _dot(u, w1_ref[gi])
    lam = lamt_ref[gi]
    ar = lam[0:1, :]
    ai = lam[1:2, :]
    sgn = jnp.where(lax.broadcasted_iota(jnp.int32, (1, 2 * p), 1) < p, -1.0, 1.0)
    c_idx = lax.broadcasted_iota(jnp.int32, (n, 2 * p), 0) % chunks_per_seq
    d = 1
    while d < chunks_per_seq:
        xs = jnp.where(c_idx >= d, pltpu.roll(x, d, 0), 0.0)
        x = x + ar * xs + (sgn * ai) * pltpu.roll(xs, p, 1)
        ar, ai = ar * ar - ai * ai, 2.0 * ar * ai
        d *= 2
    x_prev = jnp.where(c_idx >= 1, pltpu.roll(x, 1, 0), 0.0)
    y = _dot(u, wbig_ref[gi, 0:width, :]) + _dot(x_prev.astype(BF16), wbig_ref[gi, width:, :])
    y_ref[gi] = y + dexp_ref[gi] * uf


def _ssm_kernel(chunks_per_seq, lsr_ref, lrr_ref, lir_ref, btr_ref, bti_ref, ctr_ref, cti_ref,
                u_ref, dexp_ref, y_ref, wbig_ref, w1_ref, lamt_ref):
    for gi in range(SSM_GROUPS_PER_STEP):
        _ssm_operators(gi, lsr_ref, lrr_ref, lir_ref, btr_ref, bti_ref, ctr_ref, cti_ref,
                       wbig_ref, w1_ref, lamt_ref)
    for gi in range(SSM_GROUPS_PER_STEP):
        _ssm_apply(gi, chunks_per_seq, u_ref, wbig_ref, w1_ref, lamt_ref, dexp_ref, y_ref)


def _ssm(u_g, log_step, lam_re, lam_im, b_re, b_im, c_re, c_im, d, chunks_per_seq):
    g, n, width = u_g.shape
    p, hh, t = SSM_STATE, SSM_GROUP, SSM_CHUNK
    row2 = lambda v: jnp.concatenate([v, v], axis=-1)[:, None, :]
    ls = jnp.broadcast_to(log_step[:, None], (g, p))
    bt = lambda b: jnp.concatenate([b.transpose(0, 2, 1)] * 2, axis=-1)
    ct = lambda c: c.transpose(0, 2, 1)
    dexp = jnp.tile(d.reshape(g, hh), (1, t))[:, None, :]
    gps = SSM_GROUPS_PER_STEP
    grp = lambda *s: pl.BlockSpec((gps,) + s, lambda i: (i,) + (0,) * len(s))
    return pl.pallas_call(
        functools.partial(_ssm_kernel, chunks_per_seq),
        grid=(g // gps,),
        in_specs=([grp(1, 2 * p)] * 3 + [grp(hh, 2 * p)] * 2 + [grp(p, hh)] * 2
                  + [grp(n, width), grp(1, width)]),
        out_specs=grp(n, width),
        out_shape=jax.ShapeDtypeStruct((g, n, width), F32),
        scratch_shapes=[pltpu.VMEM((gps, width + 2 * p, width), BF16),
                        pltpu.VMEM((gps, width, 2 * p), BF16),
                        pltpu.VMEM((gps, SUBLANES, 2 * p), F32)],
        compiler_params=_params(),
        name="ssm",
    )(row2(ls), row2(lam_re), row2(lam_im), bt(b_re), bt(b_im), ct(c_re), ct(c_im), u_g, dexp)


def _merge_kernel(hn_ref, yg_ref, rowperm_ref, perm_ref, ma_ref, gb_ref, wglu_ref,
                  wmix_ref, g_ref, b_ref, o_ref):
    t = SSM_CHUNK
    per_tile = LANES // SSM_GROUP
    n_q = D_SSM // LANES
    rows = []
    for m in range(t // per_tile):
        for q in range(n_q):
            rows.append(jnp.concatenate(
                [yg_ref[q * per_tile + gam, :, m * LANES:(m + 1) * LANES] for gam in range(per_tile)],
                axis=1))
    swapped = _dot(jnp.concatenate(rows, axis=0).astype(BF16), perm_ref[...])
    blocks = []
    for m in range(t // per_tile):
        for i0 in range(per_tile):
            blocks.append(jnp.concatenate(
                [swapped[(m * n_q + q) * SUBLANES:(m * n_q + q + 1) * SUBLANES, i0 * LANES:(i0 + 1) * LANES]
                 for q in range(n_q)], axis=1))
    y_kc = jnp.concatenate(blocks, axis=0).astype(BF16)
    ys = _dot(rowperm_ref[...], y_kc).astype(BF16)
    for r0 in range(0, ROW_TILE, SPLIT_ROWS):
        rs = slice(r0, r0 + SPLIT_ROWS)
        z = _dot(ys[rs], wglu_ref[...])
        yb = z[:, :D_MODEL] * jax.nn.sigmoid(z[:, D_MODEL:])
        merged = ma_ref[rs, :].astype(F32) + gb_ref[rs, :].astype(F32) * yb
        mix = _dot(merged.astype(BF16), wmix_ref[...])
        o_ref[rs, :] = _layer_norm(DEEPNORM_ALPHA * hn_ref[rs, :] + mix, g_ref[...], b_ref[...])


def _merge(hn, y_g, ma, gb, w_glu, w_mix, g, b):
    n = hn.shape[0]
    chunks = ROW_TILE // SSM_CHUNK
    assert chunks == SUBLANES, "one row tile must fill the sublanes of a regrouped S5 block"
    row = lambda w: pl.BlockSpec((ROW_TILE, w), lambda i: (i, 0))
    vec = _const_spec((1, D_MODEL))
    return pl.pallas_call(
        _merge_kernel,
        grid=(n // ROW_TILE,),
        in_specs=[row(D_MODEL),
                  pl.BlockSpec((SSM_GROUPS, chunks, SSM_GROUP * SSM_CHUNK), lambda i: (0, i, 0)),
                  _const_spec((ROW_TILE, ROW_TILE)), _const_spec((SUBLANES * LANES, SUBLANES * LANES)),
                  row(D_MODEL), row(D_MODEL),
                  _const_spec((D_SSM, 2 * D_MODEL)), _const_spec((D_MODEL, D_MODEL)), vec, vec],
        out_specs=row(D_MODEL),
        out_shape=jax.ShapeDtypeStruct((n, D_MODEL), F32),
        compiler_params=_params(),
        name="merge_ln1",
    )(hn, y_g, _chunk_transpose_matrix(ROW_TILE, chunks), _block_swap_matrix(), ma, gb,
      w_glu, w_mix, g, b)


def _kv_kernel(m_ref, w_ref, k_ref, v_ref):
    kv = _dot(m_ref[...].astype(BF16), w_ref[...])
    k_ref[...] = kv[:, :D_MODEL].astype(BF16)
    v_ref[...] = kv[:, D_MODEL:].astype(BF16)


def _kv(mem2, w_kv, n_mem):
    n = mem2.shape[0]
    row = lambda w: pl.BlockSpec((n_mem, w), lambda i: (i, 0))
    return pl.pallas_call(
        _kv_kernel,
        grid=(n // n_mem,),
        in_specs=[row(D_MODEL), _const_spec((D_MODEL, 2 * D_MODEL))],
        out_specs=[row(D_MODEL), row(D_MODEL)],
        out_shape=[jax.ShapeDtypeStruct((n, D_MODEL), BF16)] * 2,
        compiler_params=_params(),
        name="mem_kv",
    )(mem2, w_kv)


def _attn_kernel(h_ref, k_ref, v_ref, wq_ref, wo_ref, g_ref, b_ref, o_ref):
    h = h_ref[...]
    q = _dot(h.astype(BF16), wq_ref[...]).astype(BF16)
    heads = []
    for hd in range(XA_HEADS):
        sl = slice(hd * XA_HEAD_DIM, (hd + 1) * XA_HEAD_DIM)
        s = lax.dot_general(q[:, sl], k_ref[:, sl], (((1,), (1,)), ((), ())),
                            preferred_element_type=F32) * (XA_HEAD_DIM ** -0.5)
        e = jnp.exp(s - jnp.max(s, axis=-1, keepdims=True))
        pr = e / jnp.sum(e, axis=-1, keepdims=True)
        heads.append(_dot(pr.astype(BF16), v_ref[:, sl]).astype(BF16))
    xa = _dot(jnp.concatenate(heads, axis=1), wo_ref[...])
    o_ref[...] = _layer_norm(DEEPNORM_ALPHA * h + xa, g_ref[...], b_ref[...])


def _attn(h1, k, v, wq, wo, g, b, seq, n_mem):
    n = h1.shape[0]
    tiles_per_seq = seq // ROW_TILE
    row = pl.BlockSpec((ROW_TILE, D_MODEL), lambda i: (i, 0))
    memb = pl.BlockSpec((n_mem, D_MODEL), lambda i: (i // tiles_per_seq, 0))
    vec = _const_spec((1, D_MODEL))
    sq = _const_spec((D_MODEL, D_MODEL))
    return pl.pallas_call(
        _attn_kernel,
        grid=(n // ROW_TILE,),
        in_specs=[row, memb, memb, sq, sq, vec, vec],
        out_specs=row,
        out_shape=jax.ShapeDtypeStruct((n, D_MODEL), F32),
        compiler_params=_params(),
        name="mem_attn_ln2",
    )(h1, k, v, wq, wo, g, b)


def _mlp_kernel(h_ref, wu_ref, wd_ref, g_ref, b_ref, o_ref):
    for r0 in range(0, ROW_TILE, SPLIT_ROWS):
        rs = slice(r0, r0 + SPLIT_ROWS)
        h = h_ref[rs, :]
        hb = h.astype(BF16)
        ff = jnp.zeros((SPLIT_ROWS, D_MODEL), F32)
        for c in range(D_FF // D_MODEL):
            sl = slice(c * D_MODEL, (c + 1) * D_MODEL)
            z = jnp.maximum(_dot(hb, wu_ref[:, sl]), 0.0)
            ff = ff + _dot((z * z).astype(BF16), wd_ref[sl, :])
        o_ref[rs, :] = _layer_norm(DEEPNORM_ALPHA * h + ff, g_ref[...], b_ref[...])


def _mlp(h2, w_up, w_down, g, b):
    n = h2.shape[0]
    row = pl.BlockSpec((ROW_TILE, D_MODEL), lambda i: (i, 0))
    vec = _const_spec((1, D_MODEL))
    return pl.pallas_call(
        _mlp_kernel,
        grid=(n // ROW_TILE,),
        in_specs=[row, _const_spec((D_MODEL, D_FF)), _const_spec((D_FF, D_MODEL)), vec, vec],
        out_specs=row,
        out_shape=jax.ShapeDtypeStruct((n, D_MODEL), F32),
        compiler_params=_params(),
        name="mlp_ln3",
    )(h2, w_up, w_down, g, b)


def _layer(h_in, mem2, ing, inb, w_in, conv_dw, conv_db, conv_ng, conv_nb, w_conv_out, log_step,
           lam_re, lam_im, b_re, b_im, c_re, c_im, d, w_ssm_glu, w_mix_out, ln1_g, ln1_b,
           wq, wkv, wo, ln2_g, ln2_b, w_up, w_down, ln3_g, ln3_b, seq, n_mem):
    vec = lambda v: v.reshape(1, -1)
    ma, u_g, gb, hn = _in_conv(h_in, ing, inb, w_in.astype(BF16), conv_dw, vec(conv_db), vec(conv_ng),
                               vec(conv_nb), w_conv_out.astype(BF16), seq)
    y_g = _ssm(u_g, log_step, lam_re, lam_im, b_re, b_im, c_re, c_im, d, seq // SSM_CHUNK)
    h1 = _merge(hn, y_g, ma, gb, w_ssm_glu.astype(BF16), w_mix_out.astype(BF16),
                vec(ln1_g), vec(ln1_b))
    k, v = _kv(mem2, wkv.astype(BF16), n_mem)
    h2 = _attn(h1, k, v, wq.astype(BF16), wo.astype(BF16), vec(ln2_g), vec(ln2_b), seq, n_mem)
    return _mlp(h2, w_up.astype(BF16), w_down.astype(BF16), vec(ln3_g), vec(ln3_b))


def kernel(x, mem, in_norm_g, in_norm_b, w_in, conv_dw, conv_db, conv_norm_g, conv_norm_b, w_conv_out, ssm_log_step, ssm_lambda_re, ssm_lambda_im, ssm_b_re, ssm_b_im, ssm_c_re, ssm_c_im, ssm_d, w_ssm_glu, w_mix_out, ln1_g, ln1_b, xa_wq, xa_wkv, xa_wo, ln2_g, ln2_b, mlp_w_up, mlp_w_down, ln3_g, ln3_b):
    bsz, seq, dm = x.shape
    n_mem = mem.shape[1]
    depth = w_in.shape[0]
    assert depth == 1, "the input normalisation is fused into the single layer's kernels"
    assert dm == D_MODEL and seq % ROW_TILE == 0 and seq % SSM_CHUNK == 0
    x2 = x.reshape(bsz * seq, dm)
    mem2 = mem.reshape(bsz * n_mem, dm)
    out = _layer(x2, mem2, in_norm_g.reshape(1, -1), in_norm_b.reshape(1, -1), w_in[0], conv_dw[0],
                 conv_db[0], conv_norm_g[0], conv_norm_b[0], w_conv_out[0], ssm_log_step[0],
                 ssm_lambda_re[0], ssm_lambda_im[0], ssm_b_re[0], ssm_b_im[0], ssm_c_re[0],
                 ssm_c_im[0], ssm_d[0], w_ssm_glu[0], w_mix_out[0], ln1_g[0], ln1_b[0], xa_wq[0],
                 xa_wkv[0], xa_wo[0], ln2_g[0], ln2_b[0], mlp_w_up[0], mlp_w_down[0], ln3_g[0],
                 ln3_b[0], seq, n_mem)
    return out.reshape(bsz, seq, dm)
```

```python
import functools

import jax
import jax.numpy as jnp
from jax import lax
from jax.experimental import pallas as pl
from jax.experimental.pallas import tpu as pltpu

F32 = jnp.float32
BF16 = jnp.bfloat16

D_MODEL = 1024
D_CONV = 1024
CONV_K = 31
D_SSM = 512
SSM_GROUP = 16
SSM_GROUPS = 32
SSM_STATE = 64
XA_HEADS = 4
XA_HEAD_DIM = 256
D_FF = 4096
LN_EPS = 1e-5
DEEPNORM_ALPHA = 2.0 ** 0.25

LANES = 128
SUBLANES = 8
MXU_WIDTH = 256
SSM_CHUNK = 64
SSM_GROUPS_PER_STEP = 4
CONV_HALO = 32
CONV_ROWS = 128
CONV_TAP_GROUP = 8
ROW_TILE = 512
SPLIT_ROWS = 256
VMEM_LIMIT = 56 * 1024 * 1024


def _layer_norm(x, g, b):
    mu = jnp.mean(x, axis=-1, keepdims=True)
    xc = x - mu
    var = jnp.mean(xc * xc, axis=-1, keepdims=True)
    return xc * lax.rsqrt(var + LN_EPS) * g + b


def _dot(a, b):
    return jnp.dot(a, b, preferred_element_type=F32)


def _params(semantics="parallel"):
    return pltpu.CompilerParams(dimension_semantics=(semantics,), vmem_limit_bytes=VMEM_LIMIT)


def _const_spec(shape):
    nd = len(shape)
    return pl.BlockSpec(shape, lambda *_: (0,) * nd)


def _block_swap_matrix():
    idx = jnp.arange(SUBLANES * LANES)
    a, b, h = idx // LANES, (idx % LANES) // SSM_GROUP, idx % SSM_GROUP
    return (idx[None, :] == (b * LANES + a * SSM_GROUP + h)[:, None]).astype(BF16)


def _chunk_transpose_matrix(rows, inner):
    idx = jnp.arange(rows)
    src = (idx % (rows // inner)) * inner + idx // (rows // inner)
    return (idx[None, :] == src[:, None]).astype(BF16)


def _in_conv_kernel(tiles_per_seq, x_ref, ing_ref, inb_ref, w_ref, rowperm_ref, perm_ref, dw_ref,
                    db_ref, ng_ref, nb_ref, wco_ref, ma_ref, ug_ref, gb_ref, hn_ref, raw_ref, *bufs):
    tm = ROW_TILE
    n_col = D_CONV // LANES
    ext_ref, pk_ref = bufs[:n_col], bufs[n_col:]
    o0, o1, o2, o3 = D_CONV, 2 * D_CONV, 2 * D_CONV + D_SSM, 2 * D_CONV + D_SSM + D_MODEL
    first = (pl.program_id(0) % tiles_per_seq) == 0

    @pl.when(first)
    def _():
        for ref in ext_ref:
            ref[0:CONV_HALO, :] = jnp.zeros((CONV_HALO, LANES), F32)
            ref[CONV_HALO + tm:, :] = jnp.zeros((SUBLANES, LANES), F32)

    @pl.when(jnp.logical_not(first))
    def _():
        for ref in ext_ref:
            ref[0:CONV_HALO, :] = ref[tm:tm + CONV_HALO, :]

    hn = _layer_norm(x_ref[...], ing_ref[...], inb_ref[...])
    hn_ref[...] = hn
    h = hn.astype(BF16)

    def zero_after(v):
        bits = pltpu.bitcast(v, jnp.uint32)
        return lax.shift_right_logical(lax.shift_right_logical(bits, jnp.uint32(16)), jnp.uint32(16))

    def weights(cols_w, zero):
        w = w_ref[:, cols_w]
        if zero is None:
            return w
        head = pltpu.bitcast(pltpu.bitcast(w[0:2 * SUBLANES, 0:LANES], jnp.uint32) | zero, BF16)
        top = jnp.concatenate([head, w[0:2 * SUBLANES, LANES:]], axis=1)
        return jnp.concatenate([top, w[2 * SUBLANES:]], axis=0)

    def glu_piece(p):
        def run(zero):
            gate = _dot(h, weights(slice(o0 + p * MXU_WIDTH, o0 + (p + 1) * MXU_WIDTH), zero))
            u = _dot(h, weights(slice(p * MXU_WIDTH, (p + 1) * MXU_WIDTH), zero)) * jax.nn.sigmoid(gate)
            for half in range(MXU_WIDTH // LANES):
                ext_ref[p * (MXU_WIDTH // LANES) + half][CONV_HALO:CONV_HALO + tm, :] = (
                    u[:, half * LANES:(half + 1) * LANES])
            return gate[0:SUBLANES, 0:LANES]
        return run

    def gate_pieces(i0):
        def run(zero):
            for i in (i0, i0 + 1):
                raw = _dot(h, weights(slice(o2 + i * MXU_WIDTH, o2 + (i + 1) * MXU_WIDTH), zero))
                raw_ref[:, i * MXU_WIDTH:(i + 1) * MXU_WIDTH] = raw
            return raw[0:SUBLANES, 0:LANES]
        return run

    def ssm_input(zero):
        t = SSM_CHUNK
        per_tile = LANES // SSM_GROUP
        n_q = D_SSM // LANES
        n_c = tm // t
        s_kc = _dot(rowperm_ref[...], _dot(h, weights(slice(o1, o2), zero)).astype(BF16))
        rows = []
        for m in range(t // per_tile):
            for q in range(n_q):
                rows.append(jnp.concatenate(
                    [s_kc[(per_tile * m + k0) * n_c:(per_tile * m + k0 + 1) * n_c, q * LANES:(q + 1) * LANES]
                     for k0 in range(per_tile)], axis=1))
        swapped = _dot(jnp.concatenate(rows, axis=0).astype(BF16), perm_ref[...])
        for m in range(t // per_tile):
            for q in range(n_q):
                r0 = (m * n_q + q) * SUBLANES
                for gam in range(per_tile):
                    ug_ref[q * per_tile + gam, :, m * LANES:(m + 1) * LANES] = (
                        swapped[r0:r0 + SUBLANES, gam * LANES:(gam + 1) * LANES])
        return swapped[0:SUBLANES, 0:LANES]

    base = CONV_HALO - (CONV_K - 1)

    def conv_block(c, after):
        cs = slice(c * LANES, (c + 1) * LANES)
        taps = dw_ref[:, cs]
        if after is not None:
            zero = zero_after(after)[0:1, :]
            taps = [pltpu.bitcast(pltpu.bitcast(taps[j:j + 1, :], jnp.uint32) | zero, F32)
                    for j in range(CONV_K)]
        else:
            taps = [taps[j:j + 1, :] for j in range(CONV_K)]
        taps = [tap.astype(BF16) for tap in taps]
        n_ext = CONV_HALO + tm
        packed = []
        for shift in range(2):
            rows = ext_ref[c][shift:shift + n_ext, :].astype(BF16)
            pk_ref[2 * c + shift][...] = pltpu.bitcast(rows, jnp.uint32)
            packed.append(pk_ref[2 * c + shift])
        blocks = []
        for r0 in range(0, tm, CONV_ROWS):
            acc = jnp.broadcast_to(db_ref[:, cs], (CONV_ROWS, LANES))
            for g0 in range(0, CONV_K, CONV_TAP_GROUP):
                part = None
                for j in range(g0, min(g0 + CONV_TAP_GROUP, CONV_K)):
                    off = base + j + r0
                    words = packed[off % 2][off // 2:off // 2 + CONV_ROWS // 2, :]
                    term = pltpu.bitcast(words, BF16) * taps[j]
                    part = term if part is None else part + term
                acc = acc + part.astype(F32)
            blocks.append(acc)
        return jnp.concatenate(blocks, axis=0)

    side = [glu_piece(1), gate_pieces(0), glu_piece(2), gate_pieces(2), glu_piece(3), gate_pieces(4),
            ssm_input, gate_pieces(6)]
    assert len(side) == n_col
    glu_piece(0)(None)
    cols = []
    side_done = []
    for c in range(n_col):
        side_done.append(side[c](zero_after(cols[c - 2][0:SUBLANES, :]) if c >= 2 else None))
        cols.append(conv_block(c, side_done[c - 1] if c >= 1 else None))
    conv = _layer_norm(jnp.concatenate(cols, axis=1), ng_ref[...], nb_ref[...])
    conv = conv * jax.nn.sigmoid(conv)
    ya = _dot(conv.astype(BF16), wco_ref[...])
    ma_ref[...] = (jax.nn.sigmoid(raw_ref[:, :D_MODEL]) * ya).astype(BF16)
    gb_ref[...] = jax.nn.sigmoid(raw_ref[:, D_MODEL:]).astype(BF16)


def _in_conv(x2, ing, inb, w_in, dw, db, ng, nb, w_conv_out, seq):
    n = x2.shape[0]
    tm = ROW_TILE
    d_in = w_in.shape[1]
    chunks = tm // SSM_CHUNK
    assert chunks == SUBLANES, "one row tile must fill the sublanes of a regrouped S5 block"
    row = lambda w: pl.BlockSpec((tm, w), lambda i: (i, 0))
    vec = _const_spec((1, D_MODEL))
    return pl.pallas_call(
        functools.partial(_in_conv_kernel, seq // tm),
        grid=(n // tm,),
        in_specs=[row(D_MODEL), vec, vec, _const_spec((D_MODEL, d_in)),
                  _const_spec((tm, tm)), _const_spec((SUBLANES * LANES, SUBLANES * LANES)),
                  _const_spec((CONV_K, D_CONV)), vec, vec, vec, _const_spec((D_CONV, D_MODEL))],
        out_specs=[row(D_MODEL),
                   pl.BlockSpec((SSM_GROUPS, chunks, SSM_GROUP * SSM_CHUNK), lambda i: (0, i, 0)),
                   row(D_MODEL), row(D_MODEL)],
        out_shape=[jax.ShapeDtypeStruct((n, D_MODEL), BF16),
                   jax.ShapeDtypeStruct((SSM_GROUPS, n // SSM_CHUNK, SSM_GROUP * SSM_CHUNK), F32),
                   jax.ShapeDtypeStruct((n, D_MODEL), BF16),
                   jax.ShapeDtypeStruct((n, D_MODEL), F32)],
        scratch_shapes=([pltpu.VMEM((tm, 2 * D_MODEL), F32)]
                        + [pltpu.VMEM((CONV_HALO + tm + SUBLANES, LANES), F32)] * (D_CONV // LANES)
                        + [pltpu.VMEM(((CONV_HALO + tm) // 2, LANES), jnp.uint32)] * (2 * D_CONV // LANES)),
        compiler_params=_params("arbitrary"),
        name="in_conv",
    )(x2, ing, inb, w_in, _chunk_transpose_matrix(tm, SSM_CHUNK), _block_swap_matrix(), dw, db, ng, nb,
      w_conv_out)


def _ssm_operators(gi, lsr_ref, lrr_ref, lir_ref, btr_ref, bti_ref, ctr_ref, cti_ref,
                   wbig_ref, w1_ref, lamt_ref):
    t = SSM_CHUNK
    p = SSM_STATE
    hh = SSM_GROUP
    width = hh * t
    assert t == SUBLANES * SUBLANES
    hp = lax.Precision.HIGHEST

    step_r = jnp.exp(lsr_ref[gi])
    lr = lrr_ref[gi]
    li = lir_ref[gi]
    a_r = lr * step_r
    th_r = li * step_r

    def powers(k):
        mag = jnp.exp(a_r * k)
        return mag * jnp.cos(th_r * k), mag * jnp.sin(th_r * k)

    sub = lax.broadcasted_iota(jnp.int32, (SUBLANES, 2 * p), 0).astype(F32)
    fine_r, fine_i = powers(sub)
    coarse_r, coarse_i = powers(sub * float(SUBLANES))
    rows_r, rows_i = [], []
    for a in range(t // SUBLANES):
        cr8 = coarse_r[a:a + 1, :]
        ci8 = coarse_i[a:a + 1, :]
        rows_r.append(cr8 * fine_r - ci8 * fine_i)
        rows_i.append(cr8 * fine_i + ci8 * fine_r)
    lkr = jnp.concatenate(rows_r, axis=0)
    lki = jnp.concatenate(rows_i, axis=0)

    ar = lkr[1:2, :]
    ai = lki[1:2, :]
    den = lr * lr + li * li
    nr = ar - 1.0
    cr = (nr * lr + ai * li) / den
    ci = (ai * lr - nr * li) / den
    btr = btr_ref[gi]
    bti = bti_ref[gi]
    bbr = cr * btr - ci * bti
    bbi = cr * bti + ci * btr

    def widen(x, onehot):
        hi = x.astype(BF16)
        rest = x - hi.astype(F32)
        mid = rest.astype(BF16)
        lo = (rest - mid.astype(F32)).astype(BF16)
        return _dot(hi, onehot) + _dot(mid, onehot) + _dot(lo, onehot)

    l0r = lkr.T[0:p, :]
    l0i = lki.T[0:p, :]
    lam_r = l0r[:, 1:2]
    lam_i = l0i[:, 1:2]
    col_t = lax.broadcasted_iota(jnp.int32, (t, width), 1)
    by_step = (lax.broadcasted_iota(jnp.int32, (t, width), 0) == col_t // hh).astype(BF16)
    col_h = lax.broadcasted_iota(jnp.int32, (hh, width), 1)
    by_chan = (lax.broadcasted_iota(jnp.int32, (hh, width), 0) == col_h % hh).astype(BF16)
    l0r = widen(l0r, by_step)
    l0i = widen(l0i, by_step)
    cer = widen(ctr_ref[gi], by_chan)
    cei = widen(cti_ref[gi], by_chan)
    cl0r = cer * l0r - cei * l0i
    cl0i = cer * l0i + cei * l0r
    cl1r = cl0r * lam_r - cl0i * lam_i
    cl1i = cl0r * lam_i + cl0i * lam_r
    wbig_ref[gi, width:width + p, :] = cl1r.astype(BF16)
    wbig_ref[gi, width + p:width + 2 * p, :] = (-cl1i).astype(BF16)

    g2 = (jnp.dot(bbr[:, :p], cl0r, precision=hp, preferred_element_type=F32)
          - jnp.dot(bbi[:, :p], cl0i, precision=hp, preferred_element_type=F32))
    col = lax.broadcasted_iota(jnp.int32, (hh, width), 1)
    for k in range(t):
        taps = g2 if k == 0 else pltpu.roll(g2, k * hh, 1)
        wbig_ref[gi, k * hh:(k + 1) * hh, :] = jnp.where(col >= k * hh, taps, 0.0).astype(BF16)

    real_half = lax.broadcasted_iota(jnp.int32, (hh, 2 * p), 1) < p
    for k in range(t):
        er = lkr[t - 1 - k:t - k, :]
        ei = lki[t - 1 - k:t - k, :]
        w1_ref[gi, k * hh:(k + 1) * hh, :] = jnp.where(real_half, er * bbr - ei * bbi,
                                                      er * bbi + ei * bbr).astype(BF16)

    ltr, lti = powers(float(t))
    row = lax.broadcasted_iota(jnp.int32, (SUBLANES, 2 * p), 0)
    lamt_ref[gi] = jnp.where(row == 0, ltr, jnp.where(row == 1, lti, 0.0))


def _ssm_apply(gi, chunks_per_seq, u_ref, wbig_ref, w1_ref, lamt_ref, dexp_ref, y_ref):
    p = SSM_STATE
    width = SSM_GROUP * SSM_CHUNK
    uf = u_ref[gi]
    u = uf.astype(BF16)
    n = u.shape[0]
    x = _dot(u, w1_ref[gi])
    lam = lamt_ref[gi]
    ar = lam[0:1, :]
    ai = lam[1:2, :]
    sgn = jnp.where(lax.broadcasted_iota(jnp.int32, (1, 2 * p), 1) < p, -1.0, 1.0)
    c_idx = lax.broadcasted_iota(jnp.int32, (n, 2 * p), 0) % chunks_per_seq
    d = 1
    while d < chunks_per_seq:
        xs = jnp.where(c_idx >= d, pltpu.roll(x, d, 0), 0.0)
        x = x + ar * xs + (sgn * ai) * pltpu.roll(xs, p, 1)
        ar, ai = ar * ar - ai * ai, 2.0 * ar * ai
        d *= 2
    x_prev = jnp.where(c_idx >= 1, pltpu.roll(x, 1, 0), 0.0)
    y = _dot(u, wbig_ref[gi, 0:width, :]) + _dot(x_prev.astype(BF16), wbig_ref[gi, width:, :])
    y_ref[gi] = y + dexp_ref[gi] * uf


def _ssm_kernel(chunks_per_seq, lsr_ref, lrr_ref, lir_ref, btr_ref, bti_ref, ctr_ref, cti_ref,
                u_ref, dexp_ref, y_ref, wbig_ref, w1_ref, lamt_ref):
    for gi in range(SSM_GROUPS_PER_STEP):
        _ssm_operators(gi, lsr_ref, lrr_ref, lir_ref, btr_ref, bti_ref, ctr_ref, cti_ref,
                       wbig_ref, w1_ref, lamt_ref)
    for gi in range(SSM_GROUPS_PER_STEP):
        _ssm_apply(gi, chunks_per_seq, u_ref, wbig_ref, w1_ref, lamt_ref, dexp_ref, y_ref)


def _ssm(u_g, log_step, lam_re, lam_im, b_re, b_im, c_re, c_im, d, chunks_per_seq):
    g, n, width = u_g.shape
    p, hh, t = SSM_STATE, SSM_GROUP, SSM_CHUNK
    row2 = lambda v: jnp.concatenate([v, v], axis=-1)[:, None, :]
    ls = jnp.broadcast_to(log_step[:, None], (g, p))
    bt = lambda b: jnp.concatenate([b.transpose(0, 2, 1)] * 2, axis=-1)
    ct = lambda c: c.transpose(0, 2, 1)
    dexp = jnp.tile(d.reshape(g, hh), (1, t))[:, None, :]
    gps = SSM_GROUPS_PER_STEP
    grp = lambda *s: pl.BlockSpec((gps,) + s, lambda i: (i,) + (0,) * len(s))
    return pl.pallas_call(
        functools.partial(_ssm_kernel, chunks_per_seq),
        grid=(g // gps,),
        in_specs=([grp(1, 2 * p)] * 3 + [grp(hh, 2 * p)] * 2 + [grp(p, hh)] * 2
                  + [grp(n, width), grp(1, width)]),
        out_specs=grp(n, width),
        out_shape=jax.ShapeDtypeStruct((g, n, width), F32),
        scratch_shapes=[pltpu.VMEM((gps, width + 2 * p, width), BF16),
                        pltpu.VMEM((gps, width, 2 * p), BF16),
                        pltpu.VMEM((gps, SUBLANES, 2 * p), F32)],
        compiler_params=_params(),
        name="ssm",
    )(row2(ls), row2(lam_re), row2(lam_im), bt(b_re), bt(b_im), ct(c_re), ct(c_im), u_g, dexp)


def _merge_kernel(hn_ref, yg_ref, rowperm_ref, perm_ref, ma_ref, gb_ref, wglu_ref,
                  wmix_ref, g_ref, b_ref, o_ref):
    t = SSM_CHUNK
    per_tile = LANES // SSM_GROUP
    n_q = D_SSM // LANES
    rows = []
    for m in range(t // per_tile):
        for q in range(n_q):
            rows.append(jnp.concatenate(
                [yg_ref[q * per_tile + gam, :, m * LANES:(m + 1) * LANES] for gam in range(per_tile)],
                axis=1))
    swapped = _dot(jnp.concatenate(rows, axis=0).astype(BF16), perm_ref[...])
    blocks = []
    for m in range(t // per_tile):
        for i0 in range(per_tile):
            blocks.append(jnp.concatenate(
                [swapped[(m * n_q + q) * SUBLANES:(m * n_q + q + 1) * SUBLANES, i0 * LANES:(i0 + 1) * LANES]
                 for q in range(n_q)], axis=1))
    y_kc = jnp.concatenate(blocks, axis=0).astype(BF16)
    ys = _dot(rowperm_ref[...], y_kc).astype(BF16)
    for r0 in range(0, ROW_TILE, SPLIT_ROWS):
        rs = slice(r0, r0 + SPLIT_ROWS)
        z = _dot(ys[rs], wglu_ref[...])
        yb = z[:, :D_MODEL] * jax.nn.sigmoid(z[:, D_MODEL:])
        merged = ma_ref[rs, :].astype(F32) + gb_ref[rs, :].astype(F32) * yb
        mix = _dot(merged.astype(BF16), wmix_ref[...])
        o_ref[rs, :] = _layer_norm(DEEPNORM_ALPHA * hn_ref[rs, :] + mix, g_ref[...], b_ref[...])


def _merge(hn, y_g, ma, gb, w_glu, w_mix, g, b):
    n = hn.shape[0]
    chunks = ROW_TILE // SSM_CHUNK
    assert chunks == SUBLANES, "one row tile must fill the sublanes of a regrouped S5 block"
    row = lambda w: pl.BlockSpec((ROW_TILE, w), lambda i: (i, 0))
    vec = _const_spec((1, D_MODEL))
    return pl.pallas_call(
        _merge_kernel,
        grid=(n // ROW_TILE,),
        in_specs=[row(D_MODEL),
                  pl.BlockSpec((SSM_GROUPS, chunks, SSM_GROUP * SSM_CHUNK), lambda i: (0, i, 0)),
                  _const_spec((ROW_TILE, ROW_TILE)), _const_spec((SUBLANES * LANES, SUBLANES * LANES)),
                  row(D_MODEL), row(D_MODEL),
                  _const_spec((D_SSM, 2 * D_MODEL)), _const_spec((D_MODEL, D_MODEL)), vec, vec],
        out_specs=row(D_MODEL),
        out_shape=jax.ShapeDtypeStruct((n, D_MODEL), F32),
        compiler_params=_params(),
        name="merge_ln1",
    )(hn, y_g, _chunk_transpose_matrix(ROW_TILE, chunks), _block_swap_matrix(), ma, gb,
      w_glu, w_mix, g, b)


def _kv_kernel(m_ref, w_ref, k_ref, v_ref):
    kv = _dot(m_ref[...].astype(BF16), w_ref[...])
    k_ref[...] = kv[:, :D_MODEL].astype(BF16)
    v_ref[...] = kv[:, D_MODEL:].astype(BF16)


def _kv(mem2, w_kv, n_mem):
    n = mem2.shape[0]
    row = lambda w: pl.BlockSpec((n_mem, w), lambda i: (i, 0))
    return pl.pallas_call(
        _kv_kernel,
        grid=(n // n_mem,),
        in_specs=[row(D_MODEL), _const_spec((D_MODEL, 2 * D_MODEL))],
        out_specs=[row(D_MODEL), row(D_MODEL)],
        out_shape=[jax.ShapeDtypeStruct((n, D_MODEL), BF16)] * 2,
        compiler_params=_params(),
        name="mem_kv",
    )(mem2, w_kv)


def _attn_kernel(h_ref, k_ref, v_ref, wq_ref, wo_ref, g_ref, b_ref, o_ref):
    h = h_ref[...]
    q = _dot(h.astype(BF16), wq_ref[...]).astype(BF16)
    heads = []
    for hd in range(XA_HEADS):
        sl = slice(hd * XA_HEAD_DIM, (hd + 1) * XA_HEAD_DIM)
        s = lax.dot_general(q[:, sl], k_ref[:, sl], (((1,), (1,)), ((), ())),
                            preferred_element_type=F32) * (XA_HEAD_DIM ** -0.5)
        e = jnp.exp(s - jnp.max(s, axis=-1, keepdims=True))
        pr = e / jnp.sum(e, axis=-1, keepdims=True)
        heads.append(_dot(pr.astype(BF16), v_ref[:, sl]).astype(BF16))
    xa = _dot(jnp.concatenate(heads, axis=1), wo_ref[...])
    o_ref[...] = _layer_norm(DEEPNORM_ALPHA * h + xa, g_ref[...], b_ref[...])


def _attn(h1, k, v, wq, wo, g, b, seq, n_mem):
    n = h1.shape[0]
    tiles_per_seq = seq // ROW_TILE
    row = pl.BlockSpec((ROW_TILE, D_MODEL), lambda i: (i, 0))
    memb = pl.BlockSpec((n_mem, D_MODEL), lambda i: (i // tiles_per_seq, 0))
    vec = _const_spec((1, D_MODEL))
    sq = _const_spec((D_MODEL, D_MODEL))
    return pl.pallas_call(
        _attn_kernel,
        grid=(n // ROW_TILE,),
        in_specs=[row, memb, memb, sq, sq, vec, vec],
        out_specs=row,
        out_shape=jax.ShapeDtypeStruct((n, D_MODEL), F32),
        compiler_params=_params(),
        name="mem_attn_ln2",
    )(h1, k, v, wq, wo, g, b)


def _mlp_kernel(h_ref, wu_ref, wd_ref, g_ref, b_ref, o_ref):
    for r0 in range(0, ROW_TILE, SPLIT_ROWS):
        rs = slice(r0, r0 + SPLIT_ROWS)
        h = h_ref[rs, :]
        hb = h.astype(BF16)
        ff = jnp.zeros((SPLIT_ROWS, D_MODEL), F32)
        for c in range(D_FF // D_MODEL):
            sl = slice(c * D_MODEL, (c + 1) * D_MODEL)
            z = jnp.maximum(_dot(hb, wu_ref[:, sl]), 0.0)
            ff = ff + _dot((z * z).astype(BF16), wd_ref[sl, :])
        o_ref[rs, :] = _layer_norm(DEEPNORM_ALPHA * h + ff, g_ref[...], b_ref[...])


def _mlp(h2, w_up, w_down, g, b):
    n = h2.shape[0]
    row = pl.BlockSpec((ROW_TILE, D_MODEL), lambda i: (i, 0))
    vec = _const_spec((1, D_MODEL))
    return pl.pallas_call(
        _mlp_kernel,
        grid=(n // ROW_TILE,),
        in_specs=[row, _const_spec((D_MODEL, D_FF)), _const_spec((D_FF, D_MODEL)), vec, vec],
        out_specs=row,
        out_shape=jax.ShapeDtypeStruct((n, D_MODEL), F32),
        compiler_params=_params(),
        name="mlp_ln3",
    )(h2, w_up, w_down, g, b)


def _layer(h_in, mem2, ing, inb, w_in, conv_dw, conv_db, conv_ng, conv_nb, w_conv_out, log_step,
           lam_re, lam_im, b_re, b_im, c_re, c_im, d, w_ssm_glu, w_mix_out, ln1_g, ln1_b,
           wq, wkv, wo, ln2_g, ln2_b, w_up, w_down, ln3_g, ln3_b, seq, n_mem):
    vec = lambda v: v.reshape(1, -1)
    ma, u_g, gb, hn = _in_conv(h_in, ing, inb, w_in.astype(BF16), conv_dw, vec(conv_db), vec(conv_ng),
                               vec(conv_nb), w_conv_out.astype(BF16), seq)
    y_g = _ssm(u_g, log_step, lam_re, lam_im, b_re, b_im, c_re, c_im, d, seq // SSM_CHUNK)
    h1 = _merge(hn, y_g, ma, gb, w_ssm_glu.astype(BF16), w_mix_out.astype(BF16),
                vec(ln1_g), vec(ln1_b))
    k, v = _kv(mem2, wkv.astype(BF16), n_mem)
    h2 = _attn(h1, k, v, wq.astype(BF16), wo.astype(BF16), vec(ln2_g), vec(ln2_b), seq, n_mem)
    return _mlp(h2, w_up.astype(BF16), w_down.astype(BF16), vec(ln3_g), vec(ln3_b))


def kernel(x, mem, in_norm_g, in_norm_b, w_in, conv_dw, conv_db, conv_norm_g, conv_norm_b, w_conv_out, ssm_log_step, ssm_lambda_re, ssm_lambda_im, ssm_b_re, ssm_b_im, ssm_c_re, ssm_c_im, ssm_d, w_ssm_glu, w_mix_out, ln1_g, ln1_b, xa_wq, xa_wkv, xa_wo, ln2_g, ln2_b, mlp_w_up, mlp_w_down, ln3_g, ln3_b):
    bsz, seq, dm = x.shape
    n_mem = mem.shape[1]
    depth = w_in.shape[0]
    assert depth == 1, "the input normalisation is fused into the single layer's kernels"
    assert dm == D_MODEL and seq % ROW_TILE == 0 and seq % SSM_CHUNK == 0
    x2 = x.reshape(bsz * seq, dm)
    mem2 = mem.reshape(bsz * n_mem, dm)
    out = _layer(x2, mem2, in_norm_g.reshape(1, -1), in_norm_b.reshape(1, -1), w_in[0], conv_dw[0],
                 conv_db[0], conv_norm_g[0], conv_norm_b[0], w_conv_out[0], ssm_log_step[0],
                 ssm_lambda_re[0], ssm_lambda_im[0], ssm_b_re[0], ssm_b_im[0], ssm_c_re[0],
                 ssm_c_im[0], ssm_d[0], w_ssm_glu[0], w_mix_out[0], ln1_g[0], ln1_b[0], xa_wq[0],
                 xa_wkv[0], xa_wo[0], ln2_g[0], ln2_b[0], mlp_w_up[0], mlp_w_down[0], ln3_g[0],
                 ln3_b[0], seq, n_mem)
    return out.reshape(bsz, seq, dm)
```

```python
import functools

import jax
import jax.numpy as jnp
from jax import lax
from jax.experimental import pallas as pl
from jax.experimental.pallas import tpu as pltpu

F32 = jnp.float32
BF16 = jnp.bfloat16

D_MODEL = 1024
D_CONV = 1024
CONV_K = 31
D_SSM = 512
SSM_GROUP = 16
SSM_GROUPS = 32
SSM_STATE = 64
XA_HEADS = 4
XA_HEAD_DIM = 256
D_FF = 4096
LN_EPS = 1e-5
DEEPNORM_ALPHA = 2.0 ** 0.25

LANES = 128
SUBLANES = 8
MXU_WIDTH = 256
SSM_CHUNK = 64
SSM_GROUPS_PER_STEP = 4
CONV_HALO = 32
CONV_ROWS = 128
CONV_TAP_GROUP = 8
ROW_TILE = 512
SPLIT_ROWS = 256
WEIGHT_CHUNK = 512
VMEM_LIMIT = 56 * 1024 * 1024


def _layer_norm(x, g, b):
    mu = jnp.mean(x, axis=-1, keepdims=True)
    xc = x - mu
    var = jnp.mean(xc * xc, axis=-1, keepdims=True)
    return xc * lax.rsqrt(var + LN_EPS) * g + b


def _dot(a, b):
    return jnp.dot(a, b, preferred_element_type=F32)


def _params(semantics="parallel"):
    return pltpu.CompilerParams(dimension_semantics=(semantics,), vmem_limit_bytes=VMEM_LIMIT)


def _const_spec(shape):
    nd = len(shape)
    return pl.BlockSpec(shape, lambda *_: (0,) * nd)


def _block_swap_matrix():
    idx = jnp.arange(SUBLANES * LANES)
    a, b, h = idx // LANES, (idx % LANES) // SSM_GROUP, idx % SSM_GROUP
    return (idx[None, :] == (b * LANES + a * SSM_GROUP + h)[:, None]).astype(BF16)


def _chunk_transpose_matrix(rows, inner):
    idx = jnp.arange(rows)
    src = (idx % (rows // inner)) * inner + idx // (rows // inner)
    return (idx[None, :] == src[:, None]).astype(BF16)


def _in_conv_kernel(tiles_per_seq, x_ref, ing_ref, inb_ref, w_ref, rowperm_ref, perm_ref, dw_ref,
                    db_ref, ng_ref, nb_ref, wco_ref, ma_ref, ug_ref, gb_ref, hn_ref, raw_ref, *bufs):
    tm = ROW_TILE
    n_col = D_CONV // LANES
    ext_ref, pk_ref = bufs[:n_col], bufs[n_col:]
    o0, o1, o2, o3 = D_CONV, 2 * D_CONV, 2 * D_CONV + D_SSM, 2 * D_CONV + D_SSM + D_MODEL
    first = (pl.program_id(0) % tiles_per_seq) == 0

    @pl.when(first)
    def _():
        for ref in ext_ref:
            ref[0:CONV_HALO, :] = jnp.zeros((CONV_HALO, LANES), F32)
            ref[CONV_HALO + tm:, :] = jnp.zeros((SUBLANES, LANES), F32)

    @pl.when(jnp.logical_not(first))
    def _():
        for ref in ext_ref:
            ref[0:CONV_HALO, :] = ref[tm:tm + CONV_HALO, :]

    hn = _layer_norm(x_ref[...], ing_ref[...], inb_ref[...])
    hn_ref[...] = hn
    h = hn.astype(BF16)

    def zero_after(v):
        bits = pltpu.bitcast(v, jnp.uint32)
        return lax.shift_right_logical(lax.shift_right_logical(bits, jnp.uint32(16)), jnp.uint32(16))

    def glu_piece(p):
        def run():
            gate = _dot(h, w_ref[:, o0 + p * MXU_WIDTH:o0 + (p + 1) * MXU_WIDTH])
            u = _dot(h, w_ref[:, p * MXU_WIDTH:(p + 1) * MXU_WIDTH]) * jax.nn.sigmoid(gate)
            for half in range(MXU_WIDTH // LANES):
                ext_ref[p * (MXU_WIDTH // LANES) + half][CONV_HALO:CONV_HALO + tm, :] = (
                    u[:, half * LANES:(half + 1) * LANES])
            return gate[0:SUBLANES, 0:LANES]
        return run

    def gate_pieces(i0):
        def run():
            for i in (i0, i0 + 1):
                raw = _dot(h, w_ref[:, o2 + i * MXU_WIDTH:o2 + (i + 1) * MXU_WIDTH])
                raw_ref[:, i * MXU_WIDTH:(i + 1) * MXU_WIDTH] = raw
            return raw[0:SUBLANES, 0:LANES]
        return run

    def ssm_input():
        t = SSM_CHUNK
        per_tile = LANES // SSM_GROUP
        n_q = D_SSM // LANES
        n_c = tm // t
        s_kc = _dot(rowperm_ref[...], _dot(h, w_ref[:, o1:o2]).astype(BF16))
        rows = []
        for m in range(t // per_tile):
            for q in range(n_q):
                rows.append(jnp.concatenate(
                    [s_kc[(per_tile * m + k0) * n_c:(per_tile * m + k0 + 1) * n_c, q * LANES:(q + 1) * LANES]
                     for k0 in range(per_tile)], axis=1))
        swapped = _dot(jnp.concatenate(rows, axis=0).astype(BF16), perm_ref[...])
        for m in range(t // per_tile):
            for q in range(n_q):
                r0 = (m * n_q + q) * SUBLANES
                for gam in range(per_tile):
                    ug_ref[q * per_tile + gam, :, m * LANES:(m + 1) * LANES] = (
                        swapped[r0:r0 + SUBLANES, gam * LANES:(gam + 1) * LANES])
        return swapped[0:SUBLANES, 0:LANES]

    base = CONV_HALO - (CONV_K - 1)

    def conv_block(c, after):
        cs = slice(c * LANES, (c + 1) * LANES)
        taps = dw_ref[:, cs]
        if after is not None:
            zero = zero_after(after)[0:1, :]
            taps = [pltpu.bitcast(pltpu.bitcast(taps[j:j + 1, :], jnp.uint32) | zero, F32)
                    for j in range(CONV_K)]
        else:
            taps = [taps[j:j + 1, :] for j in range(CONV_K)]
        taps = [tap.astype(BF16) for tap in taps]
        n_ext = CONV_HALO + tm
        packed = []
        for shift in range(2):
            rows = ext_ref[c][shift:shift + n_ext, :].astype(BF16)
            pk_ref[2 * c + shift][...] = pltpu.bitcast(rows, jnp.uint32)
            packed.append(pk_ref[2 * c + shift])
        blocks = []
        for r0 in range(0, tm, CONV_ROWS):
            acc = jnp.broadcast_to(db_ref[:, cs], (CONV_ROWS, LANES))
            for g0 in range(0, CONV_K, CONV_TAP_GROUP):
                part = None
                for j in range(g0, min(g0 + CONV_TAP_GROUP, CONV_K)):
                    off = base + j + r0
                    words = packed[off % 2][off // 2:off // 2 + CONV_ROWS // 2, :]
                    term = pltpu.bitcast(words, BF16) * taps[j]
                    part = term if part is None else part + term
                acc = acc + part.astype(F32)
            blocks.append(acc)
        return jnp.concatenate(blocks, axis=0)

    side = [glu_piece(1), gate_pieces(0), glu_piece(2), gate_pieces(2), glu_piece(3), gate_pieces(4),
            ssm_input, gate_pieces(6)]
    assert len(side) == n_col
    glu_piece(0)()
    cols = []
    side_done = []
    for c in range(n_col):
        side_done.append(side[c]())
        cols.append(conv_block(c, side_done[c - 1] if c >= 1 else None))
    conv = _layer_norm(jnp.concatenate(cols, axis=1), ng_ref[...], nb_ref[...])
    conv = conv * jax.nn.sigmoid(conv)
    ya = _dot(conv.astype(BF16), wco_ref[...])
    ma_ref[...] = (jax.nn.sigmoid(raw_ref[:, :D_MODEL]) * ya).astype(BF16)
    gb_ref[...] = jax.nn.sigmoid(raw_ref[:, D_MODEL:]).astype(BF16)


def _in_conv(x2, ing, inb, w_in, dw, db, ng, nb, w_conv_out, seq):
    n = x2.shape[0]
    tm = ROW_TILE
    d_in = w_in.shape[1]
    chunks = tm // SSM_CHUNK
    assert chunks == SUBLANES, "one row tile must fill the sublanes of a regrouped S5 block"
    row = lambda w: pl.BlockSpec((tm, w), lambda i: (i, 0))
    vec = _const_spec((1, D_MODEL))
    return pl.pallas_call(
        functools.partial(_in_conv_kernel, seq // tm),
        grid=(n // tm,),
        in_specs=[row(D_MODEL), vec, vec, _const_spec((D_MODEL, d_in)),
                  _const_spec((tm, tm)), _const_spec((SUBLANES * LANES, SUBLANES * LANES)),
                  _const_spec((CONV_K, D_CONV)), vec, vec, vec, _const_spec((D_CONV, D_MODEL))],
        out_specs=[row(D_MODEL),
                   pl.BlockSpec((SSM_GROUPS, chunks, SSM_GROUP * SSM_CHUNK), lambda i: (0, i, 0)),
                   row(D_MODEL), row(D_MODEL)],
        out_shape=[jax.ShapeDtypeStruct((n, D_MODEL), BF16),
                   jax.ShapeDtypeStruct((SSM_GROUPS, n // SSM_CHUNK, SSM_GROUP * SSM_CHUNK), F32),
                   jax.ShapeDtypeStruct((n, D_MODEL), BF16),
                   jax.ShapeDtypeStruct((n, D_MODEL), F32)],
        scratch_shapes=([pltpu.VMEM((tm, 2 * D_MODEL), F32)]
                        + [pltpu.VMEM((CONV_HALO + tm + SUBLANES, LANES), F32)] * (D_CONV // LANES)
                        + [pltpu.VMEM(((CONV_HALO + tm) // 2, LANES), jnp.uint32)] * (2 * D_CONV // LANES)),
        compiler_params=_params("arbitrary"),
        name="in_conv",
    )(x2, ing, inb, w_in, _chunk_transpose_matrix(tm, SSM_CHUNK), _block_swap_matrix(), dw, db, ng, nb,
      w_conv_out)


def _ssm_operators(gi, lsr_ref, lrr_ref, lir_ref, btr_ref, bti_ref, ctr_ref, cti_ref,
                   wbig_ref, w1_ref, lamt_ref):
    t = SSM_CHUNK
    p = SSM_STATE
    hh = SSM_GROUP
    width = hh * t
    assert t == SUBLANES * SUBLANES
    hp = lax.Precision.HIGHEST

    step_r = jnp.exp(lsr_ref[gi])
    lr = lrr_ref[gi]
    li = lir_ref[gi]
    a_r = lr * step_r
    th_r = li * step_r

    def powers(k):
        mag = jnp.exp(a_r * k)
        return mag * jnp.cos(th_r * k), mag * jnp.sin(th_r * k)

    sub = lax.broadcasted_iota(jnp.int32, (SUBLANES, 2 * p), 0).astype(F32)
    fine_r, fine_i = powers(sub)
    coarse_r, coarse_i = powers(sub * float(SUBLANES))
    rows_r, rows_i = [], []
    for a in range(t // SUBLANES):
        cr8 = coarse_r[a:a + 1, :]
        ci8 = coarse_i[a:a + 1, :]
        rows_r.append(cr8 * fine_r - ci8 * fine_i)
        rows_i.append(cr8 * fine_i + ci8 * fine_r)
    lkr = jnp.concatenate(rows_r, axis=0)
    lki = jnp.concatenate(rows_i, axis=0)

    ar = lkr[1:2, :]
    ai = lki[1:2, :]
    den = lr * lr + li * li
    nr = ar - 1.0
    cr = (nr * lr + ai * li) / den
    ci = (ai * lr - nr * li) / den
    btr = btr_ref[gi]
    bti = bti_ref[gi]
    bbr = cr * btr - ci * bti
    bbi = cr * bti + ci * btr

    def widen(x, onehot):
        hi = x.astype(BF16)
        rest = x - hi.astype(F32)
        mid = rest.astype(BF16)
        lo = (rest - mid.astype(F32)).astype(BF16)
        return _dot(hi, onehot) + _dot(mid, onehot) + _dot(lo, onehot)

    l0r = lkr.T[0:p, :]
    l0i = lki.T[0:p, :]
    lam_r = l0r[:, 1:2]
    lam_i = l0i[:, 1:2]
    col_t = lax.broadcasted_iota(jnp.int32, (t, width), 1)
    by_step = (lax.broadcasted_iota(jnp.int32, (t, width), 0) == col_t // hh).astype(BF16)
    col_h = lax.broadcasted_iota(jnp.int32, (hh, width), 1)
    by_chan = (lax.broadcasted_iota(jnp.int32, (hh, width), 0) == col_h % hh).astype(BF16)
    l0r = widen(l0r, by_step)
    l0i = widen(l0i, by_step)
    cer = widen(ctr_ref[gi], by_chan)
    cei = widen(cti_ref[gi], by_chan)
    cl0r = cer * l0r - cei * l0i
    cl0i = cer * l0i + cei * l0r
    cl1r = cl0r * lam_r - cl0i * lam_i
    cl1i = cl0r * lam_i + cl0i * lam_r
    wbig_ref[gi, width:width + p, :] = cl1r.astype(BF16)
    wbig_ref[gi, width + p:width + 2 * p, :] = (-cl1i).astype(BF16)

    g2 = (jnp.dot(bbr[:, :p], cl0r, precision=hp, preferred_element_type=F32)
          - jnp.dot(bbi[:, :p], cl0i, precision=hp, preferred_element_type=F32))
    col = lax.broadcasted_iota(jnp.int32, (hh, width), 1)
    for k in range(t):
        taps = g2 if k == 0 else pltpu.roll(g2, k * hh, 1)
        wbig_ref[gi, k * hh:(k + 1) * hh, :] = jnp.where(col >= k * hh, taps, 0.0).astype(BF16)

    real_half = lax.broadcasted_iota(jnp.int32, (hh, 2 * p), 1) < p
    for k in range(t):
        er = lkr[t - 1 - k:t - k, :]
        ei = lki[t - 1 - k:t - k, :]
        w1_ref[gi, k * hh:(k + 1) * hh, :] = jnp.where(real_half, er * bbr - ei * bbi,
                                                      er * bbi + ei * bbr).astype(BF16)

    ltr, lti = powers(float(t))
    row = lax.broadcasted_iota(jnp.int32, (SUBLANES, 2 * p), 0)
    lamt_ref[gi] = jnp.where(row == 0, ltr, jnp.where(row == 1, lti, 0.0))


def _ssm_apply(gi, chunks_per_seq, u_ref, wbig_ref, w1_ref, lamt_ref, dexp_ref, y_ref):
    p = SSM_STATE
    width = SSM_GROUP * SSM_CHUNK
    uf = u_ref[gi]
    u = uf.astype(BF16)
    n = u.shape[0]
    x = _dot(u, w1_ref[gi])
    lam = lamt_ref[gi]
    ar = lam[0:1, :]
    ai = lam[1:2, :]
    sgn = jnp.where(lax.broadcasted_iota(jnp.int32, (1, 2 * p), 1) < p, -1.0, 1.0)
    c_idx = lax.broadcasted_iota(jnp.int32, (n, 2 * p), 0) % chunks_per_seq
    d = 1
    while d < chunks_per_seq:
        xs = jnp.where(c_idx >= d, pltpu.roll(x, d, 0), 0.0)
        x = x + ar * xs + (sgn * ai) * pltpu.roll(xs, p, 1)
        ar, ai = ar * ar - ai * ai, 2.0 * ar * ai
        d *= 2
    x_prev = jnp.where(c_idx >= 1, pltpu.roll(x, 1, 0), 0.0)
    y = _dot(u, wbig_ref[gi, 0:width, :]) + _dot(x_prev.astype(BF16), wbig_ref[gi, width:, :])
    y_ref[gi] = y + dexp_ref[gi] * uf


def _ssm_kernel(chunks_per_seq, lsr_ref, lrr_ref, lir_ref, btr_ref, bti_ref, ctr_ref, cti_ref,
                u_ref, dexp_ref, y_ref, wbig_ref, w1_ref, lamt_ref):
    for gi in range(SSM_GROUPS_PER_STEP):
        _ssm_operators(gi, lsr_ref, lrr_ref, lir_ref, btr_ref, bti_ref, ctr_ref, cti_ref,
                       wbig_ref, w1_ref, lamt_ref)
    for gi in range(SSM_GROUPS_PER_STEP):
        _ssm_apply(gi, chunks_per_seq, u_ref, wbig_ref, w1_ref, lamt_ref, dexp_ref, y_ref)


def _ssm(u_g, log_step, lam_re, lam_im, b_re, b_im, c_re, c_im, d, chunks_per_seq):
    g, n, width = u_g.shape
    p, hh, t = SSM_STATE, SSM_GROUP, SSM_CHUNK
    row2 = lambda v: jnp.concatenate([v, v], axis=-1)[:, None, :]
    ls = jnp.broadcast_to(log_step[:, None], (g, p))
    bt = lambda b: jnp.concatenate([b.transpose(0, 2, 1)] * 2, axis=-1)
    ct = lambda c: c.transpose(0, 2, 1)
    dexp = jnp.tile(d.reshape(g, hh), (1, t))[:, None, :]
    gps = SSM_GROUPS_PER_STEP
    grp = lambda *s: pl.BlockSpec((gps,) + s, lambda i: (i,) + (0,) * len(s))
    return pl.pallas_call(
        functools.partial(_ssm_kernel, chunks_per_seq),
        grid=(g // gps,),
        in_specs=([grp(1, 2 * p)] * 3 + [grp(hh, 2 * p)] * 2 + [grp(p, hh)] * 2
                  + [grp(n, width), grp(1, width)]),
        out_specs=grp(n, width),
        out_shape=jax.ShapeDtypeStruct((g, n, width), F32),
        scratch_shapes=[pltpu.VMEM((gps, width + 2 * p, width), BF16),
                        pltpu.VMEM((gps, width, 2 * p), BF16),
                        pltpu.VMEM((gps, SUBLANES, 2 * p), F32)],
        compiler_params=_params(),
        name="ssm",
    )(row2(ls), row2(lam_re), row2(lam_im), bt(b_re), bt(b_im), ct(c_re), ct(c_im), u_g, dexp)


def _merge_kernel(hn_ref, yg_ref, rowperm_ref, perm_ref, ma_ref, gb_ref, wglu_ref,
                  wmix_ref, g_ref, b_ref, o_ref):
    t = SSM_CHUNK
    per_tile = LANES // SSM_GROUP
    n_q = D_SSM // LANES
    rows = []
    for m in range(t // per_tile):
        for q in range(n_q):
            rows.append(jnp.concatenate(
                [yg_ref[q * per_tile + gam, :, m * LANES:(m + 1) * LANES] for gam in range(per_tile)],
                axis=1))
    swapped = _dot(jnp.concatenate(rows, axis=0).astype(BF16), perm_ref[...])
    blocks = []
    for m in range(t // per_tile):
        for i0 in range(per_tile):
            blocks.append(jnp.concatenate(
                [swapped[(m * n_q + q) * SUBLANES:(m * n_q + q + 1) * SUBLANES, i0 * LANES:(i0 + 1) * LANES]
                 for q in range(n_q)], axis=1))
    y_kc = jnp.concatenate(blocks, axis=0).astype(BF16)
    ys = _dot(rowperm_ref[...], y_kc).astype(BF16)
    for r0 in range(0, ROW_TILE, SPLIT_ROWS):
        rs = slice(r0, r0 + SPLIT_ROWS)
        z = _dot(ys[rs], wglu_ref[...])
        yb = z[:, :D_MODEL] * jax.nn.sigmoid(z[:, D_MODEL:])
        merged = ma_ref[rs, :].astype(F32) + gb_ref[rs, :].astype(F32) * yb
        mix = _dot(merged.astype(BF16), wmix_ref[...])
        o_ref[rs, :] = _layer_norm(DEEPNORM_ALPHA * hn_ref[rs, :] + mix, g_ref[...], b_ref[...])


def _merge(hn, y_g, ma, gb, w_glu, w_mix, g, b):
    n = hn.shape[0]
    chunks = ROW_TILE // SSM_CHUNK
    assert chunks == SUBLANES, "one row tile must fill the sublanes of a regrouped S5 block"
    row = lambda w: pl.BlockSpec((ROW_TILE, w), lambda i: (i, 0))
    vec = _const_spec((1, D_MODEL))
    return pl.pallas_call(
        _merge_kernel,
        grid=(n // ROW_TILE,),
        in_specs=[row(D_MODEL),
                  pl.BlockSpec((SSM_GROUPS, chunks, SSM_GROUP * SSM_CHUNK), lambda i: (0, i, 0)),
                  _const_spec((ROW_TILE, ROW_TILE)), _const_spec((SUBLANES * LANES, SUBLANES * LANES)),
                  row(D_MODEL), row(D_MODEL),
                  _const_spec((D_SSM, 2 * D_MODEL)), _const_spec((D_MODEL, D_MODEL)), vec, vec],
        out_specs=row(D_MODEL),
        out_shape=jax.ShapeDtypeStruct((n, D_MODEL), F32),
        compiler_params=_params(),
        name="merge_ln1",
    )(hn, y_g, _chunk_transpose_matrix(ROW_TILE, chunks), _block_swap_matrix(), ma, gb,
      w_glu, w_mix, g, b)


def _kv_kernel(m_ref, w_ref, k_ref, v_ref):
    kv = _dot(m_ref[...].astype(BF16), w_ref[...])
    k_ref[...] = kv[:, :D_MODEL].astype(BF16)
    v_ref[...] = kv[:, D_MODEL:].astype(BF16)


def _kv(mem2, w_kv, n_mem):
    n = mem2.shape[0]
    row = lambda w: pl.BlockSpec((n_mem, w), lambda i: (i, 0))
    return pl.pallas_call(
        _kv_kernel,
        grid=(n // n_mem,),
        in_specs=[row(D_MODEL), _const_spec((D_MODEL, 2 * D_MODEL))],
        out_specs=[row(D_MODEL), row(D_MODEL)],
        out_shape=[jax.ShapeDtypeStruct((n, D_MODEL), BF16)] * 2,
        compiler_params=_params(),
        name="mem_kv",
    )(mem2, w_kv)


def _attn_kernel(h_ref, k_ref, v_ref, wq_ref, wo_ref, g_ref, b_ref, o_ref):
    h = h_ref[...]
    q = _dot(h.astype(BF16), wq_ref[...]).astype(BF16)
    heads = []
    for hd in range(XA_HEADS):
        sl = slice(hd * XA_HEAD_DIM, (hd + 1) * XA_HEAD_DIM)
        s = lax.dot_general(q[:, sl], k_ref[:, sl], (((1,), (1,)), ((), ())),
                            preferred_element_type=F32) * (XA_HEAD_DIM ** -0.5)
        e = jnp.exp(s - jnp.max(s, axis=-1, keepdims=True))
        pr = e / jnp.sum(e, axis=-1, keepdims=True)
        heads.append(_dot(pr.astype(BF16), v_ref[:, sl]).astype(BF16))
    xa = _dot(jnp.concatenate(heads, axis=1), wo_ref[...])
    o_ref[...] = _layer_norm(DEEPNORM_ALPHA * h + xa, g_ref[...], b_ref[...])


def _attn(h1, k, v, wq, wo, g, b, seq, n_mem):
    n = h1.shape[0]
    tiles_per_seq = seq // ROW_TILE
    row = pl.BlockSpec((ROW_TILE, D_MODEL), lambda i: (i, 0))
    memb = pl.BlockSpec((n_mem, D_MODEL), lambda i: (i // tiles_per_seq, 0))
    vec = _const_spec((1, D_MODEL))
    sq = _const_spec((D_MODEL, D_MODEL))
    return pl.pallas_call(
        _attn_kernel,
        grid=(n // ROW_TILE,),
        in_specs=[row, memb, memb, sq, sq, vec, vec],
        out_specs=row,
        out_shape=jax.ShapeDtypeStruct((n, D_MODEL), F32),
        compiler_params=_params(),
        name="mem_attn_ln2",
    )(h1, k, v, wq, wo, g, b)


def _mlp_kernel(h_ref, wu_hbm, wd_hbm, g_ref, b_ref, o_ref, wu_ref, wd_ref, su_ref, sd_ref, sem):
    n_chunks = D_FF // WEIGHT_CHUNK

    def up_copy(c):
        return pltpu.make_async_copy(wu_hbm.at[:, pl.ds(c * WEIGHT_CHUNK, WEIGHT_CHUNK)],
                                     su_ref.at[c % 2], sem.at[c % 2])

    def down_copy(c):
        return pltpu.make_async_copy(wd_hbm.at[pl.ds(c * WEIGHT_CHUNK, WEIGHT_CHUNK), :],
                                     sd_ref.at[c % 2], sem.at[2 + c % 2])

    @pl.when(pl.program_id(0) == 0)
    def _():
        up_copy(0).start()
        down_copy(0).start()
        for c in range(n_chunks):
            if c + 1 < n_chunks:
                up_copy(c + 1).start()
                down_copy(c + 1).start()
            cs = slice(c * WEIGHT_CHUNK, (c + 1) * WEIGHT_CHUNK)
            up_copy(c).wait()
            wu_ref[:, cs] = su_ref[c % 2].astype(BF16)
            down_copy(c).wait()
            wd_ref[cs, :] = sd_ref[c % 2].astype(BF16)

    for r0 in range(0, ROW_TILE, SPLIT_ROWS):
        rs = slice(r0, r0 + SPLIT_ROWS)
        h = h_ref[rs, :]
        hb = h.astype(BF16)
        ff = jnp.zeros((SPLIT_ROWS, D_MODEL), F32)
        for c in range(D_FF // D_MODEL):
            sl = slice(c * D_MODEL, (c + 1) * D_MODEL)
            z = jnp.maximum(_dot(hb, wu_ref[:, sl]), 0.0)
            ff = ff + _dot((z * z).astype(BF16), wd_ref[sl, :])
        o_ref[rs, :] = _layer_norm(DEEPNORM_ALPHA * h + ff, g_ref[...], b_ref[...])


def _mlp(h2, w_up, w_down, g, b):
    n = h2.shape[0]
    row = pl.BlockSpec((ROW_TILE, D_MODEL), lambda i: (i, 0))
    vec = _const_spec((1, D_MODEL))
    return pl.pallas_call(
        _mlp_kernel,
        grid=(n // ROW_TILE,),
        in_specs=[row, pl.BlockSpec(memory_space=pl.ANY), pl.BlockSpec(memory_space=pl.ANY), vec, vec],
        out_specs=row,
        out_shape=jax.ShapeDtypeStruct((n, D_MODEL), F32),
        scratch_shapes=[pltpu.VMEM((D_MODEL, D_FF), BF16), pltpu.VMEM((D_FF, D_MODEL), BF16),
                        pltpu.VMEM((2, D_MODEL, WEIGHT_CHUNK), F32),
                        pltpu.VMEM((2, WEIGHT_CHUNK, D_MODEL), F32),
                        pltpu.SemaphoreType.DMA((4,))],
        compiler_params=_params("arbitrary"),
        name="mlp_ln3",
    )(h2, w_up, w_down, g, b)


def _layer(h_in, mem2, ing, inb, w_in, conv_dw, conv_db, conv_ng, conv_nb, w_conv_out, log_step,
           lam_re, lam_im, b_re, b_im, c_re, c_im, d, w_ssm_glu, w_mix_out, ln1_g, ln1_b,
           wq, wkv, wo, ln2_g, ln2_b, w_up, w_down, ln3_g, ln3_b, seq, n_mem):
    vec = lambda v: v.reshape(1, -1)
    ma, u_g, gb, hn = _in_conv(h_in, ing, inb, w_in.astype(BF16), conv_dw, vec(conv_db), vec(conv_ng),
                               vec(conv_nb), w_conv_out.astype(BF16), seq)
    y_g = _ssm(u_g, log_step, lam_re, lam_im, b_re, b_im, c_re, c_im, d, seq // SSM_CHUNK)
    h1 = _merge(hn, y_g, ma, gb, w_ssm_glu.astype(BF16), w_mix_out.astype(BF16),
                vec(ln1_g), vec(ln1_b))
    k, v = _kv(mem2, wkv.astype(BF16), n_mem)
    h2 = _attn(h1, k, v, wq.astype(BF16), wo.astype(BF16), vec(ln2_g), vec(ln2_b), seq, n_mem)
    return _mlp(h2, w_up, w_down, vec(ln3_g), vec(ln3_b))


def kernel(x, mem, in_norm_g, in_norm_b, w_in, conv_dw, conv_db, conv_norm_g, conv_norm_b, w_conv_out, ssm_log_step, ssm_lambda_re, ssm_lambda_im, ssm_b_re, ssm_b_im, ssm_c_re, ssm_c_im, ssm_d, w_ssm_glu, w_mix_out, ln1_g, ln1_b, xa_wq, xa_wkv, xa_wo, ln2_g, ln2_b, mlp_w_up, mlp_w_down, ln3_g, ln3_b):
    bsz, seq, dm = x.shape
    n_mem = mem.shape[1]
    depth = w_in.shape[0]
    assert depth == 1, "the input normalisation is fused into the single layer's kernels"
    assert dm == D_MODEL and seq % ROW_TILE == 0 and seq % SSM_CHUNK == 0
    x2 = x.reshape(bsz * seq, dm)
    mem2 = mem.reshape(bsz * n_mem, dm)
    out = _layer(x2, mem2, in_norm_g.reshape(1, -1), in_norm_b.reshape(1, -1), w_in[0], conv_dw[0],
                 conv_db[0], conv_norm_g[0], conv_norm_b[0], w_conv_out[0], ssm_log_step[0],
                 ssm_lambda_re[0], ssm_lambda_im[0], ssm_b_re[0], ssm_b_im[0], ssm_c_re[0],
                 ssm_c_im[0], ssm_d[0], w_ssm_glu[0], w_mix_out[0], ln1_g[0], ln1_b[0], xa_wq[0],
                 xa_wkv[0], xa_wo[0], ln2_g[0], ln2_b[0], mlp_w_up[0], mlp_w_down[0], ln3_g[0],
                 ln3_b[0], seq, n_mem)
    return out.reshape(bsz, seq, dm)
```

```python
import functools

import jax
import jax.numpy as jnp
from jax import lax
from jax.experimental import pallas as pl
from jax.experimental.pallas import tpu as pltpu

F32 = jnp.float32
BF16 = jnp.bfloat16

D_MODEL = 1024
D_CONV = 1024
CONV_K = 31
D_SSM = 512
SSM_GROUP = 16
SSM_GROUPS = 32
SSM_STATE = 64
XA_HEADS = 4
XA_HEAD_DIM = 256
D_FF = 4096
LN_EPS = 1e-5
DEEPNORM_ALPHA = 2.0 ** 0.25

LANES = 128
SUBLANES = 8
MXU_WIDTH = 256
SSM_CHUNK = 64
SSM_GROUPS_PER_STEP = 4
CONV_HALO = 32
CONV_ROWS = 128
CONV_TAP_GROUP = 8
ROW_TILE = 512
SPLIT_ROWS = 256
WEIGHT_CHUNK = 512
VMEM_LIMIT = 56 * 1024 * 1024


def _layer_norm(x, g, b):
    mu = jnp.mean(x, axis=-1, keepdims=True)
    xc = x - mu
    var = jnp.mean(xc * xc, axis=-1, keepdims=True)
    return xc * lax.rsqrt(var + LN_EPS) * g + b


def _dot(a, b):
    return jnp.dot(a, b, preferred_element_type=F32)


def _params(semantics="parallel"):
    return pltpu.CompilerParams(dimension_semantics=(semantics,), vmem_limit_bytes=VMEM_LIMIT)


def _const_spec(shape):
    nd = len(shape)
    return pl.BlockSpec(shape, lambda *_: (0,) * nd)


def _block_swap_matrix():
    idx = jnp.arange(SUBLANES * LANES)
    a, b, h = idx // LANES, (idx % LANES) // SSM_GROUP, idx % SSM_GROUP
    return (idx[None, :] == (b * LANES + a * SSM_GROUP + h)[:, None]).astype(BF16)


def _chunk_transpose_matrix(rows, inner):
    idx = jnp.arange(rows)
    src = (idx % (rows // inner)) * inner + idx // (rows // inner)
    return (idx[None, :] == src[:, None]).astype(BF16)


def _in_conv_kernel(tiles_per_seq, x_ref, ing_ref, inb_ref, w_hbm, rowperm_ref, perm_ref, dw_ref,
                    db_ref, ng_ref, nb_ref, wco_ref, ma_ref, ug_ref, gb_ref, hn_ref, raw_ref, w_ref,
                    stage_ref, sem, *bufs):
    tm = ROW_TILE
    n_col = D_CONV // LANES
    ext_ref, pk_ref = bufs[:n_col], bufs[n_col:]

    def w_copy(c):
        return pltpu.make_async_copy(w_hbm.at[:, pl.ds(c * WEIGHT_CHUNK, WEIGHT_CHUNK)],
                                     stage_ref.at[c % 2], sem.at[c % 2])

    @pl.when(pl.program_id(0) == 0)
    def _():
        n_chunks = w_ref.shape[1] // WEIGHT_CHUNK
        w_copy(0).start()
        for c in range(n_chunks):
            if c + 1 < n_chunks:
                w_copy(c + 1).start()
            w_copy(c).wait()
            w_ref[:, c * WEIGHT_CHUNK:(c + 1) * WEIGHT_CHUNK] = stage_ref[c % 2].astype(BF16)

    o0, o1, o2, o3 = D_CONV, 2 * D_CONV, 2 * D_CONV + D_SSM, 2 * D_CONV + D_SSM + D_MODEL
    first = (pl.program_id(0) % tiles_per_seq) == 0

    @pl.when(first)
    def _():
        for ref in ext_ref:
            ref[0:CONV_HALO, :] = jnp.zeros((CONV_HALO, LANES), F32)
            ref[CONV_HALO + tm:, :] = jnp.zeros((SUBLANES, LANES), F32)

    @pl.when(jnp.logical_not(first))
    def _():
        for ref in ext_ref:
            ref[0:CONV_HALO, :] = ref[tm:tm + CONV_HALO, :]

    hn = _layer_norm(x_ref[...], ing_ref[...], inb_ref[...])
    hn_ref[...] = hn
    h = hn.astype(BF16)

    def zero_after(v):
        bits = pltpu.bitcast(v, jnp.uint32)
        return lax.shift_right_logical(lax.shift_right_logical(bits, jnp.uint32(16)), jnp.uint32(16))

    def glu_piece(p):
        def run():
            gate = _dot(h, w_ref[:, o0 + p * MXU_WIDTH:o0 + (p + 1) * MXU_WIDTH])
            u = _dot(h, w_ref[:, p * MXU_WIDTH:(p + 1) * MXU_WIDTH]) * jax.nn.sigmoid(gate)
            for half in range(MXU_WIDTH // LANES):
                ext_ref[p * (MXU_WIDTH // LANES) + half][CONV_HALO:CONV_HALO + tm, :] = (
                    u[:, half * LANES:(half + 1) * LANES])
            return gate[0:SUBLANES, 0:LANES]
        return run

    def gate_pieces(i0):
        def run():
            for i in (i0, i0 + 1):
                raw = _dot(h, w_ref[:, o2 + i * MXU_WIDTH:o2 + (i + 1) * MXU_WIDTH])
                raw_ref[:, i * MXU_WIDTH:(i + 1) * MXU_WIDTH] = raw
            return raw[0:SUBLANES, 0:LANES]
        return run

    def ssm_input():
        t = SSM_CHUNK
        per_tile = LANES // SSM_GROUP
        n_q = D_SSM // LANES
        n_c = tm // t
        s_kc = _dot(rowperm_ref[...], _dot(h, w_ref[:, o1:o2]).astype(BF16))
        rows = []
        for m in range(t // per_tile):
            for q in range(n_q):
                rows.append(jnp.concatenate(
                    [s_kc[(per_tile * m + k0) * n_c:(per_tile * m + k0 + 1) * n_c, q * LANES:(q + 1) * LANES]
                     for k0 in range(per_tile)], axis=1))
        swapped = _dot(jnp.concatenate(rows, axis=0).astype(BF16), perm_ref[...])
        for m in range(t // per_tile):
            for q in range(n_q):
                r0 = (m * n_q + q) * SUBLANES
                for gam in range(per_tile):
                    ug_ref[q * per_tile + gam, :, m * LANES:(m + 1) * LANES] = (
                        swapped[r0:r0 + SUBLANES, gam * LANES:(gam + 1) * LANES])
        return swapped[0:SUBLANES, 0:LANES]

    base = CONV_HALO - (CONV_K - 1)

    def conv_block(c, after):
        cs = slice(c * LANES, (c + 1) * LANES)
        taps = dw_ref[:, cs]
        if after is not None:
            zero = zero_after(after)[0:1, :]
            taps = [pltpu.bitcast(pltpu.bitcast(taps[j:j + 1, :], jnp.uint32) | zero, F32)
                    for j in range(CONV_K)]
        else:
            taps = [taps[j:j + 1, :] for j in range(CONV_K)]
        taps = [tap.astype(BF16) for tap in taps]
        n_ext = CONV_HALO + tm
        packed = []
        for shift in range(2):
            rows = ext_ref[c][shift:shift + n_ext, :].astype(BF16)
            pk_ref[2 * c + shift][...] = pltpu.bitcast(rows, jnp.uint32)
            packed.append(pk_ref[2 * c + shift])
        blocks = []
        for r0 in range(0, tm, CONV_ROWS):
            acc = jnp.broadcast_to(db_ref[:, cs], (CONV_ROWS, LANES))
            for g0 in range(0, CONV_K, CONV_TAP_GROUP):
                part = None
                for j in range(g0, min(g0 + CONV_TAP_GROUP, CONV_K)):
                    off = base + j + r0
                    words = packed[off % 2][off // 2:off // 2 + CONV_ROWS // 2, :]
                    term = pltpu.bitcast(words, BF16) * taps[j]
                    part = term if part is None else part + term
                acc = acc + part.astype(F32)
            blocks.append(acc)
        return jnp.concatenate(blocks, axis=0)

    side = [glu_piece(1), gate_pieces(0), glu_piece(2), gate_pieces(2), glu_piece(3), gate_pieces(4),
            ssm_input, gate_pieces(6)]
    assert len(side) == n_col
    glu_piece(0)()
    cols = []
    side_done = []
    for c in range(n_col):
        side_done.append(side[c]())
        cols.append(conv_block(c, side_done[c - 1] if c >= 1 else None))
    conv = _layer_norm(jnp.concatenate(cols, axis=1), ng_ref[...], nb_ref[...])
    conv = conv * jax.nn.sigmoid(conv)
    ya = _dot(conv.astype(BF16), wco_ref[...])
    ma_ref[...] = (jax.nn.sigmoid(raw_ref[:, :D_MODEL]) * ya).astype(BF16)
    gb_ref[...] = jax.nn.sigmoid(raw_ref[:, D_MODEL:]).astype(BF16)


def _in_conv(x2, ing, inb, w_in, dw, db, ng, nb, w_conv_out, seq):
    n = x2.shape[0]
    tm = ROW_TILE
    d_in = w_in.shape[1]
    chunks = tm // SSM_CHUNK
    assert chunks == SUBLANES, "one row tile must fill the sublanes of a regrouped S5 block"
    row = lambda w: pl.BlockSpec((tm, w), lambda i: (i, 0))
    vec = _const_spec((1, D_MODEL))
    return pl.pallas_call(
        functools.partial(_in_conv_kernel, seq // tm),
        grid=(n // tm,),
        in_specs=[row(D_MODEL), vec, vec, pl.BlockSpec(memory_space=pl.ANY),
                  _const_spec((tm, tm)), _const_spec((SUBLANES * LANES, SUBLANES * LANES)),
                  _const_spec((CONV_K, D_CONV)), vec, vec, vec, _const_spec((D_CONV, D_MODEL))],
        out_specs=[row(D_MODEL),
                   pl.BlockSpec((SSM_GROUPS, chunks, SSM_GROUP * SSM_CHUNK), lambda i: (0, i, 0)),
                   row(D_MODEL), row(D_MODEL)],
        out_shape=[jax.ShapeDtypeStruct((n, D_MODEL), BF16),
                   jax.ShapeDtypeStruct((SSM_GROUPS, n // SSM_CHUNK, SSM_GROUP * SSM_CHUNK), F32),
                   jax.ShapeDtypeStruct((n, D_MODEL), BF16),
                   jax.ShapeDtypeStruct((n, D_MODEL), F32)],
        scratch_shapes=([pltpu.VMEM((tm, 2 * D_MODEL), F32), pltpu.VMEM((D_MODEL, d_in), BF16),
                         pltpu.VMEM((2, D_MODEL, WEIGHT_CHUNK), F32), pltpu.SemaphoreType.DMA((2,))]
                        + [pltpu.VMEM((CONV_HALO + tm + SUBLANES, LANES), F32)] * (D_CONV // LANES)
                        + [pltpu.VMEM(((CONV_HALO + tm) // 2, LANES), jnp.uint32)] * (2 * D_CONV // LANES)),
        compiler_params=_params("arbitrary"),
        name="in_conv",
    )(x2, ing, inb, w_in, _chunk_transpose_matrix(tm, SSM_CHUNK), _block_swap_matrix(), dw, db, ng, nb,
      w_conv_out)


def _ssm_operators(gi, lsr_ref, lrr_ref, lir_ref, btr_ref, bti_ref, ctr_ref, cti_ref,
                   wbig_ref, w1_ref, lamt_ref):
    t = SSM_CHUNK
    p = SSM_STATE
    hh = SSM_GROUP
    width = hh * t
    assert t == SUBLANES * SUBLANES
    hp = lax.Precision.HIGHEST

    step_r = jnp.exp(lsr_ref[gi])
    lr = lrr_ref[gi]
    li = lir_ref[gi]
    a_r = lr * step_r
    th_r = li * step_r

    def powers(k):
        mag = jnp.exp(a_r * k)
        return mag * jnp.cos(th_r * k), mag * jnp.sin(th_r * k)

    sub = lax.broadcasted_iota(jnp.int32, (SUBLANES, 2 * p), 0).astype(F32)
    fine_r, fine_i = powers(sub)
    coarse_r, coarse_i = powers(sub * float(SUBLANES))
    rows_r, rows_i = [], []
    for a in range(t // SUBLANES):
        cr8 = coarse_r[a:a + 1, :]
        ci8 = coarse_i[a:a + 1, :]
        rows_r.append(cr8 * fine_r - ci8 * fine_i)
        rows_i.append(cr8 * fine_i + ci8 * fine_r)
    lkr = jnp.concatenate(rows_r, axis=0)
    lki = jnp.concatenate(rows_i, axis=0)

    ar = lkr[1:2, :]
    ai = lki[1:2, :]
    den = lr * lr + li * li
    nr = ar - 1.0
    cr = (nr * lr + ai * li) / den
    ci = (ai * lr - nr * li) / den
    btr = btr_ref[gi]
    bti = bti_ref[gi]
    bbr = cr * btr - ci * bti
    bbi = cr * bti + ci * btr

    def widen(x, onehot):
        hi = x.astype(BF16)
        rest = x - hi.astype(F32)
        mid = rest.astype(BF16)
        lo = (rest - mid.astype(F32)).astype(BF16)
        return _dot(hi, onehot) + _dot(mid, onehot) + _dot(lo, onehot)

    l0r = lkr.T[0:p, :]
    l0i = lki.T[0:p, :]
    lam_r = l0r[:, 1:2]
    lam_i = l0i[:, 1:2]
    col_t = lax.broadcasted_iota(jnp.int32, (t, width), 1)
    by_step = (lax.broadcasted_iota(jnp.int32, (t, width), 0) == col_t // hh).astype(BF16)
    col_h = lax.broadcasted_iota(jnp.int32, (hh, width), 1)
    by_chan = (lax.broadcasted_iota(jnp.int32, (hh, width), 0) == col_h % hh).astype(BF16)
    l0r = widen(l0r, by_step)
    l0i = widen(l0i, by_step)
    cer = widen(ctr_ref[gi], by_chan)
    cei = widen(cti_ref[gi], by_chan)
    cl0r = cer * l0r - cei * l0i
    cl0i = cer * l0i + cei * l0r
    cl1r = cl0r * lam_r - cl0i * lam_i
    cl1i = cl0r * lam_i + cl0i * lam_r
    wbig_ref[gi, width:width + p, :] = cl1r.astype(BF16)
    wbig_ref[gi, width + p:width + 2 * p, :] = (-cl1i).astype(BF16)

    g2 = (jnp.dot(bbr[:, :p], cl0r, precision=hp, preferred_element_type=F32)
          - jnp.dot(bbi[:, :p], cl0i, precision=hp, preferred_element_type=F32))
    col = lax.broadcasted_iota(jnp.int32, (hh, width), 1)
    for k in range(t):
        taps = g2 if k == 0 else pltpu.roll(g2, k * hh, 1)
        wbig_ref[gi, k * hh:(k + 1) * hh, :] = jnp.where(col >= k * hh, taps, 0.0).astype(BF16)

    real_half = lax.broadcasted_iota(jnp.int32, (hh, 2 * p), 1) < p
    for k in range(t):
        er = lkr[t - 1 - k:t - k, :]
        ei = lki[t - 1 - k:t - k, :]
        w1_ref[gi, k * hh:(k + 1) * hh, :] = jnp.where(real_half, er * bbr - ei * bbi,
                                                      er * bbi + ei * bbr).astype(BF16)

    ltr, lti = powers(float(t))
    row = lax.broadcasted_iota(jnp.int32, (SUBLANES, 2 * p), 0)
    lamt_ref[gi] = jnp.where(row == 0, ltr, jnp.where(row == 1, lti, 0.0))


def _ssm_apply(gi, chunks_per_seq, u_ref, wbig_ref, w1_ref, lamt_ref, dexp_ref, y_ref):
    p = SSM_STATE
    width = SSM_GROUP * SSM_CHUNK
    uf = u_ref[gi]
    u = uf.astype(BF16)
    n = u.shape[0]
    x = _dot(u, w1_ref[gi])
    lam = lamt_ref[gi]
    ar = lam[0:1, :]
    ai = lam[1:2, :]
    sgn = jnp.where(lax.broadcasted_iota(jnp.int32, (1, 2 * p), 1) < p, -1.0, 1.0)
    c_idx = lax.broadcasted_iota(jnp.int32, (n, 2 * p), 0) % chunks_per_seq
    d = 1
    while d < chunks_per_seq:
        xs = jnp.where(c_idx >= d, pltpu.roll(x, d, 0), 0.0)
        x = x + ar * xs + (sgn * ai) * pltpu.roll(xs, p, 1)
        ar, ai = ar * ar - ai * ai, 2.0 * ar * ai
        d *= 2
    x_prev = jnp.where(c_idx >= 1, pltpu.roll(x, 1, 0), 0.0)
    y = _dot(u, wbig_ref[gi, 0:width, :]) + _dot(x_prev.astype(BF16), wbig_ref[gi, width:, :])
    y_ref[gi] = y + dexp_ref[gi] * uf


def _ssm_kernel(chunks_per_seq, lsr_ref, lrr_ref, lir_ref, btr_ref, bti_ref, ctr_ref, cti_ref,
                u_ref, dexp_ref, y_ref, wbig_ref, w1_ref, lamt_ref):
    for gi in range(SSM_GROUPS_PER_STEP):
        _ssm_operators(gi, lsr_ref, lrr_ref, lir_ref, btr_ref, bti_ref, ctr_ref, cti_ref,
                       wbig_ref, w1_ref, lamt_ref)
    for gi in range(SSM_GROUPS_PER_STEP):
        _ssm_apply(gi, chunks_per_seq, u_ref, wbig_ref, w1_ref, lamt_ref, dexp_ref, y_ref)


def _ssm(u_g, log_step, lam_re, lam_im, b_re, b_im, c_re, c_im, d, chunks_per_seq):
    g, n, width = u_g.shape
    p, hh, t = SSM_STATE, SSM_GROUP, SSM_CHUNK
    row2 = lambda v: jnp.concatenate([v, v], axis=-1)[:, None, :]
    ls = jnp.broadcast_to(log_step[:, None], (g, p))
    bt = lambda b: jnp.concatenate([b.transpose(0, 2, 1)] * 2, axis=-1)
    ct = lambda c: c.transpose(0, 2, 1)
    dexp = jnp.tile(d.reshape(g, hh), (1, t))[:, None, :]
    gps = SSM_GROUPS_PER_STEP
    grp = lambda *s: pl.BlockSpec((gps,) + s, lambda i: (i,) + (0,) * len(s))
    return pl.pallas_call(
        functools.partial(_ssm_kernel, chunks_per_seq),
        grid=(g // gps,),
        in_specs=([grp(1, 2 * p)] * 3 + [grp(hh, 2 * p)] * 2 + [grp(p, hh)] * 2
                  + [grp(n, width), grp(1, width)]),
        out_specs=grp(n, width),
        out_shape=jax.ShapeDtypeStruct((g, n, width), F32),
        scratch_shapes=[pltpu.VMEM((gps, width + 2 * p, width), BF16),
                        pltpu.VMEM((gps, width, 2 * p), BF16),
                        pltpu.VMEM((gps, SUBLANES, 2 * p), F32)],
        compiler_params=_params(),
        name="ssm",
    )(row2(ls), row2(lam_re), row2(lam_im), bt(b_re), bt(b_im), ct(c_re), ct(c_im), u_g, dexp)


def _merge_kernel(hn_ref, yg_ref, rowperm_ref, perm_ref, ma_ref, gb_ref, wglu_ref,
                  wmix_ref, g_ref, b_ref, o_ref):
    t = SSM_CHUNK
    per_tile = LANES // SSM_GROUP
    n_q = D_SSM // LANES
    rows = []
    for m in range(t // per_tile):
        for q in range(n_q):
            rows.append(jnp.concatenate(
                [yg_ref[q * per_tile + gam, :, m * LANES:(m + 1) * LANES] for gam in range(per_tile)],
                axis=1))
    swapped = _dot(jnp.concatenate(rows, axis=0).astype(BF16), perm_ref[...])
    blocks = []
    for m in range(t // per_tile):
        for i0 in range(per_tile):
            blocks.append(jnp.concatenate(
                [swapped[(m * n_q + q) * SUBLANES:(m * n_q + q + 1) * SUBLANES, i0 * LANES:(i0 + 1) * LANES]
                 for q in range(n_q)], axis=1))
    y_kc = jnp.concatenate(blocks, axis=0).astype(BF16)
    ys = _dot(rowperm_ref[...], y_kc).astype(BF16)
    for r0 in range(0, ROW_TILE, SPLIT_ROWS):
        rs = slice(r0, r0 + SPLIT_ROWS)
        z = _dot(ys[rs], wglu_ref[...])
        yb = z[:, :D_MODEL] * jax.nn.sigmoid(z[:, D_MODEL:])
        merged = ma_ref[rs, :].astype(F32) + gb_ref[rs, :].astype(F32) * yb
        mix = _dot(merged.astype(BF16), wmix_ref[...])
        o_ref[rs, :] = _layer_norm(DEEPNORM_ALPHA * hn_ref[rs, :] + mix, g_ref[...], b_ref[...])


def _merge(hn, y_g, ma, gb, w_glu, w_mix, g, b):
    n = hn.shape[0]
    chunks = ROW_TILE // SSM_CHUNK
    assert chunks == SUBLANES, "one row tile must fill the sublanes of a regrouped S5 block"
    row = lambda w: pl.BlockSpec((ROW_TILE, w), lambda i: (i, 0))
    vec = _const_spec((1, D_MODEL))
    return pl.pallas_call(
        _merge_kernel,
        grid=(n // ROW_TILE,),
        in_specs=[row(D_MODEL),
                  pl.BlockSpec((SSM_GROUPS, chunks, SSM_GROUP * SSM_CHUNK), lambda i: (0, i, 0)),
                  _const_spec((ROW_TILE, ROW_TILE)), _const_spec((SUBLANES * LANES, SUBLANES * LANES)),
                  row(D_MODEL), row(D_MODEL),
                  _const_spec((D_SSM, 2 * D_MODEL)), _const_spec((D_MODEL, D_MODEL)), vec, vec],
        out_specs=row(D_MODEL),
        out_shape=jax.ShapeDtypeStruct((n, D_MODEL), F32),
        compiler_params=_params(),
        name="merge_ln1",
    )(hn, y_g, _chunk_transpose_matrix(ROW_TILE, chunks), _block_swap_matrix(), ma, gb,
      w_glu, w_mix, g, b)


def _kv_kernel(m_ref, w_ref, k_ref, v_ref):
    kv = _dot(m_ref[...].astype(BF16), w_ref[...])
    k_ref[...] = kv[:, :D_MODEL].astype(BF16)
    v_ref[...] = kv[:, D_MODEL:].astype(BF16)


def _kv(mem2, w_kv, n_mem):
    n = mem2.shape[0]
    row = lambda w: pl.BlockSpec((n_mem, w), lambda i: (i, 0))
    return pl.pallas_call(
        _kv_kernel,
        grid=(n // n_mem,),
        in_specs=[row(D_MODEL), _const_spec((D_MODEL, 2 * D_MODEL))],
        out_specs=[row(D_MODEL), row(D_MODEL)],
        out_shape=[jax.ShapeDtypeStruct((n, D_MODEL), BF16)] * 2,
        compiler_params=_params(),
        name="mem_kv",
    )(mem2, w_kv)


def _attn_kernel(h_ref, k_ref, v_ref, wq_ref, wo_ref, g_ref, b_ref, o_ref):
    h = h_ref[...]
    q = _dot(h.astype(BF16), wq_ref[...]).astype(BF16)
    heads = []
    for hd in range(XA_HEADS):
        sl = slice(hd * XA_HEAD_DIM, (hd + 1) * XA_HEAD_DIM)
        s = lax.dot_general(q[:, sl], k_ref[:, sl], (((1,), (1,)), ((), ())),
                            preferred_element_type=F32) * (XA_HEAD_DIM ** -0.5)
        e = jnp.exp(s - jnp.max(s, axis=-1, keepdims=True))
        pr = e / jnp.sum(e, axis=-1, keepdims=True)
        heads.append(_dot(pr.astype(BF16), v_ref[:, sl]).astype(BF16))
    xa = _dot(jnp.concatenate(heads, axis=1), wo_ref[...])
    o_ref[...] = _layer_norm(DEEPNORM_ALPHA * h + xa, g_ref[...], b_ref[...])


def _attn(h1, k, v, wq, wo, g, b, seq, n_mem):
    n = h1.shape[0]
    tiles_per_seq = seq // ROW_TILE
    row = pl.BlockSpec((ROW_TILE, D_MODEL), lambda i: (i, 0))
    memb = pl.BlockSpec((n_mem, D_MODEL), lambda i: (i // tiles_per_seq, 0))
    vec = _const_spec((1, D_MODEL))
    sq = _const_spec((D_MODEL, D_MODEL))
    return pl.pallas_call(
        _attn_kernel,
        grid=(n // ROW_TILE,),
        in_specs=[row, memb, memb, sq, sq, vec, vec],
        out_specs=row,
        out_shape=jax.ShapeDtypeStruct((n, D_MODEL), F32),
        compiler_params=_params(),
        name="mem_attn_ln2",
    )(h1, k, v, wq, wo, g, b)


def _mlp_kernel(h_ref, wu_hbm, wd_hbm, g_ref, b_ref, o_ref, wu_ref, wd_ref, su_ref, sd_ref, sem):
    n_chunks = D_FF // WEIGHT_CHUNK

    def up_copy(c):
        return pltpu.make_async_copy(wu_hbm.at[:, pl.ds(c * WEIGHT_CHUNK, WEIGHT_CHUNK)],
                                     su_ref.at[c % 2], sem.at[c % 2])

    def down_copy(c):
        return pltpu.make_async_copy(wd_hbm.at[pl.ds(c * WEIGHT_CHUNK, WEIGHT_CHUNK), :],
                                     sd_ref.at[c % 2], sem.at[2 + c % 2])

    @pl.when(pl.program_id(0) == 0)
    def _():
        up_copy(0).start()
        down_copy(0).start()
        for c in range(n_chunks):
            if c + 1 < n_chunks:
                up_copy(c + 1).start()
                down_copy(c + 1).start()
            cs = slice(c * WEIGHT_CHUNK, (c + 1) * WEIGHT_CHUNK)
            up_copy(c).wait()
            wu_ref[:, cs] = su_ref[c % 2].astype(BF16)
            down_copy(c).wait()
            wd_ref[cs, :] = sd_ref[c % 2].astype(BF16)

    for r0 in range(0, ROW_TILE, SPLIT_ROWS):
        rs = slice(r0, r0 + SPLIT_ROWS)
        h = h_ref[rs, :]
        hb = h.astype(BF16)
        ff = jnp.zeros((SPLIT_ROWS, D_MODEL), F32)
        for c in range(D_FF // D_MODEL):
            sl = slice(c * D_MODEL, (c + 1) * D_MODEL)
            z = jnp.maximum(_dot(hb, wu_ref[:, sl]), 0.0)
            ff = ff + _dot((z * z).astype(BF16), wd_ref[sl, :])
        o_ref[rs, :] = _layer_norm(DEEPNORM_ALPHA * h + ff, g_ref[...], b_ref[...])


def _mlp(h2, w_up, w_down, g, b):
    n = h2.shape[0]
    row = pl.BlockSpec((ROW_TILE, D_MODEL), lambda i: (i, 0))
    vec = _const_spec((1, D_MODEL))
    return pl.pallas_call(
        _mlp_kernel,
        grid=(n // ROW_TILE,),
        in_specs=[row, pl.BlockSpec(memory_space=pl.ANY), pl.BlockSpec(memory_space=pl.ANY), vec, vec],
        out_specs=row,
        out_shape=jax.ShapeDtypeStruct((n, D_MODEL), F32),
        scratch_shapes=[pltpu.VMEM((D_MODEL, D_FF), BF16), pltpu.VMEM((D_FF, D_MODEL), BF16),
                        pltpu.VMEM((2, D_MODEL, WEIGHT_CHUNK), F32),
                        pltpu.VMEM((2, WEIGHT_CHUNK, D_MODEL), F32),
                        pltpu.SemaphoreType.DMA((4,))],
        compiler_params=_params("arbitrary"),
        name="mlp_ln3",
    )(h2, w_up, w_down, g, b)


def _layer(h_in, mem2, ing, inb, w_in, conv_dw, conv_db, conv_ng, conv_nb, w_conv_out, log_step,
           lam_re, lam_im, b_re, b_im, c_re, c_im, d, w_ssm_glu, w_mix_out, ln1_g, ln1_b,
           wq, wkv, wo, ln2_g, ln2_b, w_up, w_down, ln3_g, ln3_b, seq, n_mem):
    vec = lambda v: v.reshape(1, -1)
    ma, u_g, gb, hn = _in_conv(h_in, ing, inb, w_in, conv_dw, vec(conv_db), vec(conv_ng),
                               vec(conv_nb), w_conv_out.astype(BF16), seq)
    y_g = _ssm(u_g, log_step, lam_re, lam_im, b_re, b_im, c_re, c_im, d, seq // SSM_CHUNK)
    h1 = _merge(hn, y_g, ma, gb, w_ssm_glu.astype(BF16), w_mix_out.astype(BF16),
                vec(ln1_g), vec(ln1_b))
    k, v = _kv(mem2, wkv.astype(BF16), n_mem)
    h2 = _attn(h1, k, v, wq.astype(BF16), wo.astype(BF16), vec(ln2_g), vec(ln2_b), seq, n_mem)
    return _mlp(h2, w_up, w_down, vec(ln3_g), vec(ln3_b))


def kernel(x, mem, in_norm_g, in_norm_b, w_in, conv_dw, conv_db, conv_norm_g, conv_norm_b, w_conv_out, ssm_log_step, ssm_lambda_re, ssm_lambda_im, ssm_b_re, ssm_b_im, ssm_c_re, ssm_c_im, ssm_d, w_ssm_glu, w_mix_out, ln1_g, ln1_b, xa_wq, xa_wkv, xa_wo, ln2_g, ln2_b, mlp_w_up, mlp_w_down, ln3_g, ln3_b):
    bsz, seq, dm = x.shape
    n_mem = mem.shape[1]
    depth = w_in.shape[0]
    assert depth == 1, "the input normalisation is fused into the single layer's kernels"
    assert dm == D_MODEL and seq % ROW_TILE == 0 and seq % SSM_CHUNK == 0
    x2 = x.reshape(bsz * seq, dm)
    mem2 = mem.reshape(bsz * n_mem, dm)
    out = _layer(x2, mem2, in_norm_g.reshape(1, -1), in_norm_b.reshape(1, -1), w_in[0], conv_dw[0],
                 conv_db[0], conv_norm_g[0], conv_norm_b[0], w_conv_out[0], ssm_log_step[0],
                 ssm_lambda_re[0], ssm_lambda_im[0], ssm_b_re[0], ssm_b_im[0], ssm_c_re[0],
                 ssm_c_im[0], ssm_d[0], w_ssm_glu[0], w_mix_out[0], ln1_g[0], ln1_b[0], xa_wq[0],
                 xa_wkv[0], xa_wo[0], ln2_g[0], ln2_b[0], mlp_w_up[0], mlp_w_down[0], ln3_g[0],
                 ln3_b[0], seq, n_mem)
    return out.reshape(bsz, seq, dm)
```

```python
import functools

import jax
import jax.numpy as jnp
from jax import lax
from jax.experimental import pallas as pl
from jax.experimental.pallas import tpu as pltpu

F32 = jnp.float32
BF16 = jnp.bfloat16

D_MODEL = 1024
D_CONV = 1024
CONV_K = 31
D_SSM = 512
SSM_GROUP = 16
SSM_GROUPS = 32
SSM_STATE = 64
XA_HEADS = 4
XA_HEAD_DIM = 256
D_FF = 4096
LN_EPS = 1e-5
DEEPNORM_ALPHA = 2.0 ** 0.25

LANES = 128
SUBLANES = 8
MXU_WIDTH = 256
SSM_CHUNK = 64
SSM_GROUPS_PER_STEP = 4
CONV_HALO = 32
CONV_ROWS = 128
CONV_TAP_GROUP = 8
ROW_TILE = 512
SPLIT_ROWS = 256
WEIGHT_CHUNK = 512
VMEM_LIMIT = 56 * 1024 * 1024


def _layer_norm(x, g, b):
    mu = jnp.mean(x, axis=-1, keepdims=True)
    xc = x - mu
    var = jnp.mean(xc * xc, axis=-1, keepdims=True)
    return xc * lax.rsqrt(var + LN_EPS) * g + b


def _dot(a, b):
    return jnp.dot(a, b, preferred_element_type=F32)


def _params(semantics="parallel"):
    return pltpu.CompilerParams(dimension_semantics=(semantics,), vmem_limit_bytes=VMEM_LIMIT)


def _const_spec(shape):
    nd = len(shape)
    return pl.BlockSpec(shape, lambda *_: (0,) * nd)


def _block_swap_matrix():
    idx = jnp.arange(SUBLANES * LANES)
    a, b, h = idx // LANES, (idx % LANES) // SSM_GROUP, idx % SSM_GROUP
    return (idx[None, :] == (b * LANES + a * SSM_GROUP + h)[:, None]).astype(BF16)


def _chunk_transpose_matrix(rows, inner):
    idx = jnp.arange(rows)
    src = (idx % (rows // inner)) * inner + idx // (rows // inner)
    return (idx[None, :] == src[:, None]).astype(BF16)


def _in_conv_kernel(tiles_per_seq, x_ref, ing_ref, inb_ref, w_ref, rowperm_ref, perm_ref, dw_ref,
                    db_ref, ng_ref, nb_ref, wco_ref, ma_ref, ug_ref, gb_ref, hn_ref, raw_ref, *bufs):
    tm = ROW_TILE
    n_col = D_CONV // LANES
    ext_ref, pk_ref = bufs[:n_col], bufs[n_col:]
    o0, o1, o2, o3 = D_CONV, 2 * D_CONV, 2 * D_CONV + D_SSM, 2 * D_CONV + D_SSM + D_MODEL
    first = (pl.program_id(0) % tiles_per_seq) == 0

    @pl.when(first)
    def _():
        for ref in ext_ref:
            ref[0:CONV_HALO, :] = jnp.zeros((CONV_HALO, LANES), F32)
            ref[CONV_HALO + tm:, :] = jnp.zeros((SUBLANES, LANES), F32)

    @pl.when(jnp.logical_not(first))
    def _():
        for ref in ext_ref:
            ref[0:CONV_HALO, :] = ref[tm:tm + CONV_HALO, :]

    hn = _layer_norm(x_ref[...], ing_ref[...], inb_ref[...])
    hn_ref[...] = hn
    h = hn.astype(BF16)

    def zero_after(v):
        bits = pltpu.bitcast(v, jnp.uint32)
        return lax.shift_right_logical(lax.shift_right_logical(bits, jnp.uint32(16)), jnp.uint32(16))

    def glu_piece(p):
        def run():
            gate = _dot(h, w_ref[:, o0 + p * MXU_WIDTH:o0 + (p + 1) * MXU_WIDTH])
            u = _dot(h, w_ref[:, p * MXU_WIDTH:(p + 1) * MXU_WIDTH]) * jax.nn.sigmoid(gate)
            for half in range(MXU_WIDTH // LANES):
                ext_ref[p * (MXU_WIDTH // LANES) + half][CONV_HALO:CONV_HALO + tm, :] = (
                    u[:, half * LANES:(half + 1) * LANES])
            return gate[0:SUBLANES, 0:LANES]
        return run

    def gate_pieces(i0):
        def run():
            for i in (i0, i0 + 1):
                raw = _dot(h, w_ref[:, o2 + i * MXU_WIDTH:o2 + (i + 1) * MXU_WIDTH])
                raw_ref[:, i * MXU_WIDTH:(i + 1) * MXU_WIDTH] = raw
            return raw[0:SUBLANES, 0:LANES]
        return run

    def ssm_input():
        t = SSM_CHUNK
        per_tile = LANES // SSM_GROUP
        n_q = D_SSM // LANES
        n_c = tm // t
        s_kc = _dot(rowperm_ref[...], _dot(h, w_ref[:, o1:o2]).astype(BF16))
        rows = []
        for m in range(t // per_tile):
            for q in range(n_q):
                rows.append(jnp.concatenate(
                    [s_kc[(per_tile * m + k0) * n_c:(per_tile * m + k0 + 1) * n_c, q * LANES:(q + 1) * LANES]
                     for k0 in range(per_tile)], axis=1))
        swapped = _dot(jnp.concatenate(rows, axis=0).astype(BF16), perm_ref[...])
        for m in range(t // per_tile):
            for q in range(n_q):
                r0 = (m * n_q + q) * SUBLANES
                for gam in range(per_tile):
                    ug_ref[q * per_tile + gam, :, m * LANES:(m + 1) * LANES] = (
                        swapped[r0:r0 + SUBLANES, gam * LANES:(gam + 1) * LANES])
        return swapped[0:SUBLANES, 0:LANES]

    base = CONV_HALO - (CONV_K - 1)

    def conv_block(c, after):
        cs = slice(c * LANES, (c + 1) * LANES)
        taps = dw_ref[:, cs]
        if after is not None:
            zero = zero_after(after)[0:1, :]
            taps = [pltpu.bitcast(pltpu.bitcast(taps[j:j + 1, :], jnp.uint32) | zero, F32)
                    for j in range(CONV_K)]
        else:
            taps = [taps[j:j + 1, :] for j in range(CONV_K)]
        taps = [tap.astype(BF16) for tap in taps]
        n_ext = CONV_HALO + tm
        packed = []
        for shift in range(2):
            rows = ext_ref[c][shift:shift + n_ext, :].astype(BF16)
            pk_ref[2 * c + shift][...] = pltpu.bitcast(rows, jnp.uint32)
            packed.append(pk_ref[2 * c + shift])
        blocks = []
        for r0 in range(0, tm, CONV_ROWS):
            acc = jnp.broadcast_to(db_ref[:, cs], (CONV_ROWS, LANES))
            for g0 in range(0, CONV_K, CONV_TAP_GROUP):
                part = None
                for j in range(g0, min(g0 + CONV_TAP_GROUP, CONV_K)):
                    off = base + j + r0
                    words = packed[off % 2][off // 2:off // 2 + CONV_ROWS // 2, :]
                    term = pltpu.bitcast(words, BF16) * taps[j]
                    part = term if part is None else part + term
                acc = acc + part.astype(F32)
            blocks.append(acc)
        return jnp.concatenate(blocks, axis=0)

    side = [glu_piece(1), gate_pieces(0), glu_piece(2), gate_pieces(2), glu_piece(3), gate_pieces(4),
            ssm_input, gate_pieces(6)]
    assert len(side) == n_col
    glu_piece(0)()
    cols = []
    side_done = []
    for c in range(n_col):
        side_done.append(side[c]())
        cols.append(conv_block(c, side_done[c - 1] if c >= 1 else None))
    conv = _layer_norm(jnp.concatenate(cols, axis=1), ng_ref[...], nb_ref[...])
    conv = conv * jax.nn.sigmoid(conv)
    ya = _dot(conv.astype(BF16), wco_ref[...])
    ma_ref[...] = (jax.nn.sigmoid(raw_ref[:, :D_MODEL]) * ya).astype(BF16)
    gb_ref[...] = jax.nn.sigmoid(raw_ref[:, D_MODEL:]).astype(BF16)


def _in_conv(x2, ing, inb, w_in, dw, db, ng, nb, w_conv_out, seq):
    n = x2.shape[0]
    tm = ROW_TILE
    d_in = w_in.shape[1]
    chunks = tm // SSM_CHUNK
    assert chunks == SUBLANES, "one row tile must fill the sublanes of a regrouped S5 block"
    row = lambda w: pl.BlockSpec((tm, w), lambda i: (i, 0))
    vec = _const_spec((1, D_MODEL))
    return pl.pallas_call(
        functools.partial(_in_conv_kernel, seq // tm),
        grid=(n // tm,),
        in_specs=[row(D_MODEL), vec, vec, _const_spec((D_MODEL, d_in)),
                  _const_spec((tm, tm)), _const_spec((SUBLANES * LANES, SUBLANES * LANES)),
                  _const_spec((CONV_K, D_CONV)), vec, vec, vec, _const_spec((D_CONV, D_MODEL))],
        out_specs=[row(D_MODEL),
                   pl.BlockSpec((SSM_GROUPS, chunks, SSM_GROUP * SSM_CHUNK), lambda i: (0, i, 0)),
                   row(D_MODEL), row(D_MODEL)],
        out_shape=[jax.ShapeDtypeStruct((n, D_MODEL), BF16),
                   jax.ShapeDtypeStruct((SSM_GROUPS, n // SSM_CHUNK, SSM_GROUP * SSM_CHUNK), F32),
                   jax.ShapeDtypeStruct((n, D_MODEL), BF16),
                   jax.ShapeDtypeStruct((n, D_MODEL), F32)],
        scratch_shapes=([pltpu.VMEM((tm, 2 * D_MODEL), F32)]
                        + [pltpu.VMEM((CONV_HALO + tm + SUBLANES, LANES), F32)] * (D_CONV // LANES)
                        + [pltpu.VMEM(((CONV_HALO + tm) // 2, LANES), jnp.uint32)] * (2 * D_CONV // LANES)),
        compiler_params=_params("arbitrary"),
        name="in_conv",
    )(x2, ing, inb, w_in, _chunk_transpose_matrix(tm, SSM_CHUNK), _block_swap_matrix(), dw, db, ng, nb,
      w_conv_out)


def _ssm_operators(gi, lsr_ref, lrr_ref, lir_ref, btr_ref, bti_ref, ctr_ref, cti_ref,
                   wbig_ref, w1_ref, lamt_ref):
    t = SSM_CHUNK
    p = SSM_STATE
    hh = SSM_GROUP
    width = hh * t
    assert t == SUBLANES * SUBLANES
    hp = lax.Precision.HIGHEST

    step_r = jnp.exp(lsr_ref[gi])
    lr = lrr_ref[gi]
    li = lir_ref[gi]
    a_r = lr * step_r
    th_r = li * step_r

    def powers(k):
        mag = jnp.exp(a_r * k)
        return mag * jnp.cos(th_r * k), mag * jnp.sin(th_r * k)

    sub = lax.broadcasted_iota(jnp.int32, (SUBLANES, 2 * p), 0).astype(F32)
    fine_r, fine_i = powers(sub)
    coarse_r, coarse_i = powers(sub * float(SUBLANES))
    rows_r, rows_i = [], []
    for a in range(t // SUBLANES):
        cr8 = coarse_r[a:a + 1, :]
        ci8 = coarse_i[a:a + 1, :]
        rows_r.append(cr8 * fine_r - ci8 * fine_i)
        rows_i.append(cr8 * fine_i + ci8 * fine_r)
    lkr = jnp.concatenate(rows_r, axis=0)
    lki = jnp.concatenate(rows_i, axis=0)

    ar = lkr[1:2, :]
    ai = lki[1:2, :]
    den = lr * lr + li * li
    nr = ar - 1.0
    cr = (nr * lr + ai * li) / den
    ci = (ai * lr - nr * li) / den
    btr = btr_ref[gi]
    bti = bti_ref[gi]
    bbr = cr * btr - ci * bti
    bbi = cr * bti + ci * btr

    def widen(x, onehot):
        hi = x.astype(BF16)
        rest = x - hi.astype(F32)
        mid = rest.astype(BF16)
        lo = (rest - mid.astype(F32)).astype(BF16)
        return _dot(hi, onehot) + _dot(mid, onehot) + _dot(lo, onehot)

    l0r = lkr.T[0:p, :]
    l0i = lki.T[0:p, :]
    lam_r = l0r[:, 1:2]
    lam_i = l0i[:, 1:2]
    col_t = lax.broadcasted_iota(jnp.int32, (t, width), 1)
    by_step = (lax.broadcasted_iota(jnp.int32, (t, width), 0) == col_t // hh).astype(BF16)
    col_h = lax.broadcasted_iota(jnp.int32, (hh, width), 1)
    by_chan = (lax.broadcasted_iota(jnp.int32, (hh, width), 0) == col_h % hh).astype(BF16)
    l0r = widen(l0r, by_step)
    l0i = widen(l0i, by_step)
    cer = widen(ctr_ref[gi], by_chan)
    cei = widen(cti_ref[gi], by_chan)
    cl0r = cer * l0r - cei * l0i
    cl0i = cer * l0i + cei * l0r
    cl1r = cl0r * lam_r - cl0i * lam_i
    cl1i = cl0r * lam_i + cl0i * lam_r
    wbig_ref[gi, width:width + p, :] = cl1r.astype(BF16)
    wbig_ref[gi, width + p:width + 2 * p, :] = (-cl1i).astype(BF16)

    g2 = (jnp.dot(bbr[:, :p], cl0r, precision=hp, preferred_element_type=F32)
          - jnp.dot(bbi[:, :p], cl0i, precision=hp, preferred_element_type=F32))
    col = lax.broadcasted_iota(jnp.int32, (hh, width), 1)
    for k in range(t):
        taps = g2 if k == 0 else pltpu.roll(g2, k * hh, 1)
        wbig_ref[gi, k * hh:(k + 1) * hh, :] = jnp.where(col >= k * hh, taps, 0.0).astype(BF16)

    real_half = lax.broadcasted_iota(jnp.int32, (hh, 2 * p), 1) < p
    for k in range(t):
        er = lkr[t - 1 - k:t - k, :]
        ei = lki[t - 1 - k:t - k, :]
        w1_ref[gi, k * hh:(k + 1) * hh, :] = jnp.where(real_half, er * bbr - ei * bbi,
                                                      er * bbi + ei * bbr).astype(BF16)

    ltr, lti = powers(float(t))
    row = lax.broadcasted_iota(jnp.int32, (SUBLANES, 2 * p), 0)
    lamt_ref[gi] = jnp.where(row == 0, ltr, jnp.where(row == 1, lti, 0.0))


def _ssm_apply(gi, chunks_per_seq, u_ref, wbig_ref, w1_ref, lamt_ref, dexp_ref, y_ref):
    p = SSM_STATE
    width = SSM_GROUP * SSM_CHUNK
    uf = u_ref[gi]
    u = uf.astype(BF16)
    n = u.shape[0]
    x = _dot(u, w1_ref[gi])
    lam = lamt_ref[gi]
    ar = lam[0:1, :]
    ai = lam[1:2, :]
    sgn = jnp.where(lax.broadcasted_iota(jnp.int32, (1, 2 * p), 1) < p, -1.0, 1.0)
    c_idx = lax.broadcasted_iota(jnp.int32, (n, 2 * p), 0) % chunks_per_seq
    d = 1
    while d < chunks_per_seq:
        xs = jnp.where(c_idx >= d, pltpu.roll(x, d, 0), 0.0)
        x = x + ar * xs + (sgn * ai) * pltpu.roll(xs, p, 1)
        ar, ai = ar * ar - ai * ai, 2.0 * ar * ai
        d *= 2
    x_prev = jnp.where(c_idx >= 1, pltpu.roll(x, 1, 0), 0.0)
    y = _dot(u, wbig_ref[gi, 0:width, :]) + _dot(x_prev.astype(BF16), wbig_ref[gi, width:, :])
    y_ref[gi] = y + dexp_ref[gi] * uf


def _ssm_kernel(chunks_per_seq, lsr_ref, lrr_ref, lir_ref, btr_ref, bti_ref, ctr_ref, cti_ref,
                u_ref, dexp_ref, y_ref, wbig_ref, w1_ref, lamt_ref):
    for gi in range(SSM_GROUPS_PER_STEP):
        _ssm_operators(gi, lsr_ref, lrr_ref, lir_ref, btr_ref, bti_ref, ctr_ref, cti_ref,
                       wbig_ref, w1_ref, lamt_ref)
    for gi in range(SSM_GROUPS_PER_STEP):
        _ssm_apply(gi, chunks_per_seq, u_ref, wbig_ref, w1_ref, lamt_ref, dexp_ref, y_ref)


def _ssm(u_g, log_step, lam_re, lam_im, b_re, b_im, c_re, c_im, d, chunks_per_seq):
    g, n, width = u_g.shape
    p, hh, t = SSM_STATE, SSM_GROUP, SSM_CHUNK
    row2 = lambda v: jnp.concatenate([v, v], axis=-1)[:, None, :]
    ls = jnp.broadcast_to(log_step[:, None], (g, p))
    bt = lambda b: jnp.concatenate([b.transpose(0, 2, 1)] * 2, axis=-1)
    ct = lambda c: c.transpose(0, 2, 1)
    dexp = jnp.tile(d.reshape(g, hh), (1, t))[:, None, :]
    gps = SSM_GROUPS_PER_STEP
    grp = lambda *s: pl.BlockSpec((gps,) + s, lambda i: (i,) + (0,) * len(s))
    return pl.pallas_call(
        functools.partial(_ssm_kernel, chunks_per_seq),
        grid=(g // gps,),
        in_specs=([grp(1, 2 * p)] * 3 + [grp(hh, 2 * p)] * 2 + [grp(p, hh)] * 2
                  + [grp(n, width), grp(1, width)]),
        out_specs=grp(n, width),
        out_shape=jax.ShapeDtypeStruct((g, n, width), F32),
        scratch_shapes=[pltpu.VMEM((gps, width + 2 * p, width), BF16),
                        pltpu.VMEM((gps, width, 2 * p), BF16),
                        pltpu.VMEM((gps, SUBLANES, 2 * p), F32)],
        compiler_params=_params(),
        name="ssm",
    )(row2(ls), row2(lam_re), row2(lam_im), bt(b_re), bt(b_im), ct(c_re), ct(c_im), u_g, dexp)


def _merge_kernel(hn_ref, yg_ref, rowperm_ref, perm_ref, ma_ref, gb_ref, wglu_ref,
                  wmix_ref, g_ref, b_ref, o_ref):
    t = SSM_CHUNK
    per_tile = LANES // SSM_GROUP
    n_q = D_SSM // LANES
    rows = []
    for m in range(t // per_tile):
        for q in range(n_q):
            rows.append(jnp.concatenate(
                [yg_ref[q * per_tile + gam, :, m * LANES:(m + 1) * LANES] for gam in range(per_tile)],
                axis=1))
    swapped = _dot(jnp.concatenate(rows, axis=0).astype(BF16), perm_ref[...])
    blocks = []
    for m in range(t // per_tile):
        for i0 in range(per_tile):
            blocks.append(jnp.concatenate(
                [swapped[(m * n_q + q) * SUBLANES:(m * n_q + q + 1) * SUBLANES, i0 * LANES:(i0 + 1) * LANES]
                 for q in range(n_q)], axis=1))
    y_kc = jnp.concatenate(blocks, axis=0).astype(BF16)
    ys = _dot(rowperm_ref[...], y_kc).astype(BF16)
    for r0 in range(0, ROW_TILE, SPLIT_ROWS):
        rs = slice(r0, r0 + SPLIT_ROWS)
        z = _dot(ys[rs], wglu_ref[...])
        yb = z[:, :D_MODEL] * jax.nn.sigmoid(z[:, D_MODEL:])
        merged = ma_ref[rs, :].astype(F32) + gb_ref[rs, :].astype(F32) * yb
        mix = _dot(merged.astype(BF16), wmix_ref[...])
        o_ref[rs, :] = _layer_norm(DEEPNORM_ALPHA * hn_ref[rs, :] + mix, g_ref[...], b_ref[...])


def _merge(hn, y_g, ma, gb, w_glu, w_mix, g, b):
    n = hn.shape[0]
    chunks = ROW_TILE // SSM_CHUNK
    assert chunks == SUBLANES, "one row tile must fill the sublanes of a regrouped S5 block"
    row = lambda w: pl.BlockSpec((ROW_TILE, w), lambda i: (i, 0))
    vec = _const_spec((1, D_MODEL))
    return pl.pallas_call(
        _merge_kernel,
        grid=(n // ROW_TILE,),
        in_specs=[row(D_MODEL),
                  pl.BlockSpec((SSM_GROUPS, chunks, SSM_GROUP * SSM_CHUNK), lambda i: (0, i, 0)),
                  _const_spec((ROW_TILE, ROW_TILE)), _const_spec((SUBLANES * LANES, SUBLANES * LANES)),
                  row(D_MODEL), row(D_MODEL),
                  _const_spec((D_SSM, 2 * D_MODEL)), _const_spec((D_MODEL, D_MODEL)), vec, vec],
        out_specs=row(D_MODEL),
        out_shape=jax.ShapeDtypeStruct((n, D_MODEL), F32),
        compiler_params=_params(),
        name="merge_ln1",
    )(hn, y_g, _chunk_transpose_matrix(ROW_TILE, chunks), _block_swap_matrix(), ma, gb,
      w_glu, w_mix, g, b)


def _attn_kernel(tiles_per_seq, h_ref, m_ref, wkv_ref, wq_ref, wo_ref, g_ref, b_ref, o_ref, k_ref, v_ref):
    @pl.when(pl.program_id(0) % tiles_per_seq == 0)
    def _():
        kv = _dot(m_ref[...].astype(BF16), wkv_ref[...])
        k_ref[...] = kv[:, :D_MODEL].astype(BF16)
        v_ref[...] = kv[:, D_MODEL:].astype(BF16)

    h = h_ref[...]
    q = _dot(h.astype(BF16), wq_ref[...]).astype(BF16)
    heads = []
    for hd in range(XA_HEADS):
        sl = slice(hd * XA_HEAD_DIM, (hd + 1) * XA_HEAD_DIM)
        s = lax.dot_general(q[:, sl], k_ref[:, sl], (((1,), (1,)), ((), ())),
                            preferred_element_type=F32) * (XA_HEAD_DIM ** -0.5)
        e = jnp.exp(s - jnp.max(s, axis=-1, keepdims=True))
        pr = e / jnp.sum(e, axis=-1, keepdims=True)
        heads.append(_dot(pr.astype(BF16), v_ref[:, sl]).astype(BF16))
    xa = _dot(jnp.concatenate(heads, axis=1), wo_ref[...])
    o_ref[...] = _layer_norm(DEEPNORM_ALPHA * h + xa, g_ref[...], b_ref[...])


def _attn(h1, mem2, w_kv, wq, wo, g, b, seq, n_mem):
    n = h1.shape[0]
    tiles_per_seq = seq // ROW_TILE
    row = pl.BlockSpec((ROW_TILE, D_MODEL), lambda i: (i, 0))
    memb = pl.BlockSpec((n_mem, D_MODEL), lambda i: (i // tiles_per_seq, 0))
    vec = _const_spec((1, D_MODEL))
    sq = _const_spec((D_MODEL, D_MODEL))
    return pl.pallas_call(
        functools.partial(_attn_kernel, tiles_per_seq),
        grid=(n // ROW_TILE,),
        in_specs=[row, memb, _const_spec((D_MODEL, 2 * D_MODEL)), sq, sq, vec, vec],
        out_specs=row,
        out_shape=jax.ShapeDtypeStruct((n, D_MODEL), F32),
        scratch_shapes=[pltpu.VMEM((n_mem, D_MODEL), BF16), pltpu.VMEM((n_mem, D_MODEL), BF16)],
        compiler_params=_params("arbitrary"),
        name="mem_attn_ln2",
    )(h1, mem2, w_kv, wq, wo, g, b)


def _mlp_kernel(h_ref, wu_hbm, wd_hbm, g_ref, b_ref, o_ref, wu_ref, wd_ref, su_ref, sd_ref, sem):
    n_chunks = D_FF // WEIGHT_CHUNK

    def up_copy(c):
        return pltpu.make_async_copy(wu_hbm.at[:, pl.ds(c * WEIGHT_CHUNK, WEIGHT_CHUNK)],
                                     su_ref.at[c % 2], sem.at[c % 2])

    def down_copy(c):
        return pltpu.make_async_copy(wd_hbm.at[pl.ds(c * WEIGHT_CHUNK, WEIGHT_CHUNK), :],
                                     sd_ref.at[c % 2], sem.at[2 + c % 2])

    @pl.when(pl.program_id(0) == 0)
    def _():
        up_copy(0).start()
        down_copy(0).start()
        for c in range(n_chunks):
            if c + 1 < n_chunks:
                up_copy(c + 1).start()
                down_copy(c + 1).start()
            cs = slice(c * WEIGHT_CHUNK, (c + 1) * WEIGHT_CHUNK)
            up_copy(c).wait()
            wu_ref[:, cs] = su_ref[c % 2].astype(BF16)
            down_copy(c).wait()
            wd_ref[cs, :] = sd_ref[c % 2].astype(BF16)

    for r0 in range(0, ROW_TILE, SPLIT_ROWS):
        rs = slice(r0, r0 + SPLIT_ROWS)
        h = h_ref[rs, :]
        hb = h.astype(BF16)
        ff = jnp.zeros((SPLIT_ROWS, D_MODEL), F32)
        for c in range(D_FF // D_MODEL):
            sl = slice(c * D_MODEL, (c + 1) * D_MODEL)
            z = jnp.maximum(_dot(hb, wu_ref[:, sl]), 0.0)
            ff = ff + _dot((z * z).astype(BF16), wd_ref[sl, :])
        o_ref[rs, :] = _layer_norm(DEEPNORM_ALPHA * h + ff, g_ref[...], b_ref[...])


def _mlp(h2, w_up, w_down, g, b):
    n = h2.shape[0]
    row = pl.BlockSpec((ROW_TILE, D_MODEL), lambda i: (i, 0))
    vec = _const_spec((1, D_MODEL))
    return pl.pallas_call(
        _mlp_kernel,
        grid=(n // ROW_TILE,),
        in_specs=[row, pl.BlockSpec(memory_space=pl.ANY), pl.BlockSpec(memory_space=pl.ANY), vec, vec],
        out_specs=row,
        out_shape=jax.ShapeDtypeStruct((n, D_MODEL), F32),
        scratch_shapes=[pltpu.VMEM((D_MODEL, D_FF), BF16), pltpu.VMEM((D_FF, D_MODEL), BF16),
                        pltpu.VMEM((2, D_MODEL, WEIGHT_CHUNK), F32),
                        pltpu.VMEM((2, WEIGHT_CHUNK, D_MODEL), F32),
                        pltpu.SemaphoreType.DMA((4,))],
        compiler_params=_params("arbitrary"),
        name="mlp_ln3",
    )(h2, w_up, w_down, g, b)


def _layer(h_in, mem2, ing, inb, w_in, conv_dw, conv_db, conv_ng, conv_nb, w_conv_out, log_step,
           lam_re, lam_im, b_re, b_im, c_re, c_im, d, w_ssm_glu, w_mix_out, ln1_g, ln1_b,
           wq, wkv, wo, ln2_g, ln2_b, w_up, w_down, ln3_g, ln3_b, seq, n_mem):
    vec = lambda v: v.reshape(1, -1)
    ma, u_g, gb, hn = _in_conv(h_in, ing, inb, w_in.astype(BF16), conv_dw, vec(conv_db), vec(conv_ng),
                               vec(conv_nb), w_conv_out.astype(BF16), seq)
    y_g = _ssm(u_g, log_step, lam_re, lam_im, b_re, b_im, c_re, c_im, d, seq // SSM_CHUNK)
    h1 = _merge(hn, y_g, ma, gb, w_ssm_glu.astype(BF16), w_mix_out.astype(BF16),
                vec(ln1_g), vec(ln1_b))
    h2 = _attn(h1, mem2, wkv.astype(BF16), wq.astype(BF16), wo.astype(BF16), vec(ln2_g), vec(ln2_b),
               seq, n_mem)
    return _mlp(h2, w_up, w_down, vec(ln3_g), vec(ln3_b))


def kernel(x, mem, in_norm_g, in_norm_b, w_in, conv_dw, conv_db, conv_norm_g, conv_norm_b, w_conv_out, ssm_log_step, ssm_lambda_re, ssm_lambda_im, ssm_b_re, ssm_b_im, ssm_c_re, ssm_c_im, ssm_d, w_ssm_glu, w_mix_out, ln1_g, ln1_b, xa_wq, xa_wkv, xa_wo, ln2_g, ln2_b, mlp_w_up, mlp_w_down, ln3_g, ln3_b):
    bsz, seq, dm = x.shape
    n_mem = mem.shape[1]
    depth = w_in.shape[0]
    assert depth == 1, "the input normalisation is fused into the single layer's kernels"
    assert dm == D_MODEL and seq % ROW_TILE == 0 and seq % SSM_CHUNK == 0
    x2 = x.reshape(bsz * seq, dm)
    mem2 = mem.reshape(bsz * n_mem, dm)
    out = _layer(x2, mem2, in_norm_g.reshape(1, -1), in_norm_b.reshape(1, -1), w_in[0], conv_dw[0],
                 conv_db[0], conv_norm_g[0], conv_norm_b[0], w_conv_out[0], ssm_log_step[0],
                 ssm_lambda_re[0], ssm_lambda_im[0], ssm_b_re[0], ssm_b_im[0], ssm_c_re[0],
                 ssm_c_im[0], ssm_d[0], w_ssm_glu[0], w_mix_out[0], ln1_g[0], ln1_b[0], xa_wq[0],
                 xa_wkv[0], xa_wo[0], ln2_g[0], ln2_b[0], mlp_w_up[0], mlp_w_down[0], ln3_g[0],
                 ln3_b[0], seq, n_mem)
    return out.reshape(bsz, seq, dm)
```
